```python
import math
import jax
import jax.numpy as jnp
from jax import lax
import numpy as np

D_MODEL = 1024
BATCH = 4
SEQ = 4096
DEPTH = 4
DEC_BATCH = 32
DEC_SEQ = 4
PAST_LEN = 8192
PAGE_SIZE = 128

P_DIM = 256
N_HYB = (DEPTH + 1) // 2
N_ATT = DEPTH // 2
EPS = 1e-6
ROPE_THETA = 10000.0
SSD_HEADS = 16
SSD_HEAD_DIM = 64
SSD_D = SSD_HEADS * SSD_HEAD_DIM
SSD_GROUPS = 2
SSD_STATE = 128
SSD_CONV = 4
SSD_CONV_CH = SSD_D + 2 * SSD_GROUPS * SSD_STATE
RET_HEADS = 8
RET_DK = 64
RET_DV = 128
RET_QK = RET_HEADS * RET_DK
RET_V = RET_HEADS * RET_DV
MIX_WIDTH = SSD_D + RET_V
HYB_IN = SSD_D + SSD_CONV_CH + SSD_HEADS + 2 * RET_QK + 2 * RET_V
SCAN_CHUNK = 128
ATT_HEADS = 16
ATT_HEAD_DIM = 64
ATT_D = ATT_HEADS * ATT_HEAD_DIM
MOBA_BLOCK = 256
MOBA_TOPK = 3
MOBA_Q_BLOCK = 16
FFN_DIM = 2816
N_EXPERTS = 8
TOP_K = 2
EXPERT_DIM = 1408
NEG = -1e30

kernel_name = 'hybrid_ssd_retention_moba_decoder_step'

F32 = jnp.float32


def _rmsnorm(x, g):
    xf = x.astype(F32)
    y = xf * lax.rsqrt(jnp.mean(xf * xf, axis=-1, keepdims=True) + EPS)
    return (y * g.astype(F32)).astype(x.dtype)


def _head_rms(x):
    xf = x.astype(F32)
    return xf * lax.rsqrt(jnp.mean(xf * xf, axis=-1, keepdims=True) + EPS)


def _rope(x, pos):
    half = x.shape[-1] // 2
    inv = ROPE_THETA ** (-jnp.arange(half, dtype=F32) / half)
    ang = pos.astype(F32)[:, None] * inv[None, :]
    cos = jnp.cos(ang)[:, None, :]
    sin = jnp.sin(ang)[:, None, :]
    xf = x.astype(F32)
    x1, x2 = xf[..., :half], xf[..., half:]
    return jnp.concatenate([x1 * cos - x2 * sin, x2 * cos + x1 * sin], axis=-1).astype(x.dtype)


def _decay_scan(q, k, v, log_a, s0):
    bn, l, nh, dk = q.shape
    dv = v.shape[-1]
    qlen = math.gcd(l, SCAN_CHUNK)
    nc = l // qlen
    qc = q.astype(F32).reshape(bn, nc, qlen, nh, dk)
    kc = k.astype(F32).reshape(bn, nc, qlen, nh, dk)
    vc = v.astype(F32).reshape(bn, nc, qlen, nh, dv)
    cum = jnp.cumsum(log_a.astype(F32).reshape(bn, nc, qlen, nh), axis=2)
    diff = cum[:, :, :, None, :] - cum[:, :, None, :, :]
    causal = jnp.tril(jnp.ones((qlen, qlen), dtype=bool))[None, None, :, :, None]
    decay = jnp.exp(jnp.where(causal, diff, -jnp.inf))
    scores = jnp.einsum('bcqhd,bckhd->bcqkh', qc, kc) * decay
    y_intra = jnp.einsum('bcqkh,bckhe->bcqhe', scores, vc)
    last = cum[:, :, -1]
    w_k = jnp.exp(last[:, :, None, :] - cum)
    chunk_states = jnp.einsum('bckh,bckhd,bckhe->bchde', w_k, kc, vc)
    chunk_decay = jnp.exp(last)

    def step(s, inp):
        cs, cd = inp
        return cd[..., None, None] * s + cs, s

    s_final, s_in = lax.scan(step, s0.astype(F32),
                             (jnp.moveaxis(chunk_states, 1, 0), jnp.moveaxis(chunk_decay, 1, 0)))
    s_in = jnp.moveaxis(s_in, 0, 1)
    y_inter = jnp.einsum('bcqhd,bchde->bcqhe', qc * jnp.exp(cum)[..., None], s_in)
    return (y_intra + y_inter).reshape(bn, l, nh, dv), s_final


def _hybrid_mixer(h, pos, conv_buf, s_ssm, s_ret, w_in, w_out, conv_w, conv_b, dt_bias, a_log, d_skip, ssd_g, ret_g):
    bn, l, _ = h.shape
    u = h @ w_in
    offs = np.cumsum([SSD_D, SSD_CONV_CH, SSD_HEADS, RET_QK, RET_QK, RET_V])
    z, xbc, dt_raw, rq, rk, rv, rg = jnp.split(u, [int(o) for o in offs], axis=-1)
    full = jnp.concatenate([conv_buf.astype(xbc.dtype), xbc], axis=1)
    acc = conv_b
    for j in range(SSD_CONV):
        acc = acc + full[:, j:j + l] * conv_w[j]
    xbc_c = jax.nn.silu(acc)
    new_conv = full[:, l:]
    xs, bs, cs = jnp.split(xbc_c, [SSD_D, SSD_D + SSD_GROUPS * SSD_STATE], axis=-1)
    xh = xs.reshape(bn, l, SSD_HEADS, SSD_HEAD_DIM)
    rep = SSD_HEADS // SSD_GROUPS
    bh = jnp.repeat(bs.reshape(bn, l, SSD_GROUPS, SSD_STATE), rep, axis=2)
    ch = jnp.repeat(cs.reshape(bn, l, SSD_GROUPS, SSD_STATE), rep, axis=2)
    dt = jax.nn.softplus(dt_raw.astype(F32) + dt_bias.astype(F32))
    a = -jnp.exp(a_log.astype(F32))
    y, s_ssm_new = _decay_scan(ch, bh.astype(F32) * dt[..., None], xh, dt * a, s_ssm)
    y = y + d_skip.astype(F32)[:, None] * xh.astype(F32)
    y_ssd = _rmsnorm(y.reshape(bn, l, SSD_D) * jax.nn.silu(z.astype(F32)), ssd_g)
    q = _rope(rq.reshape(bn, l, RET_HEADS, RET_DK), pos)
    k = _rope(rk.reshape(bn, l, RET_HEADS, RET_DK), pos) * (RET_DK ** -0.5)
    v = rv.reshape(bn, l, RET_HEADS, RET_DV)
    log_g = jnp.log1p(-jnp.exp2(-5.0 - jnp.arange(RET_HEADS, dtype=F32)))
    yr, s_ret_new = _decay_scan(q, k, v, jnp.broadcast_to(log_g, (bn, l, RET_HEADS)), s_ret)
    yr = _head_rms(yr).reshape(bn, l, RET_V) * ret_g.astype(F32) * jax.nn.silu(rg.astype(F32))
    out = jnp.concatenate([y_ssd.astype(F32), yr], axis=-1).astype(h.dtype) @ w_out
    return out, s_ssm_new, new_conv, s_ret_new


def _att_qkv(h, pos, w_in, qn, kn):
    bn, l, _ = h.shape
    q, k, v = jnp.split(h @ w_in, 3, axis=-1)
    shp = (bn, l, ATT_HEADS, ATT_HEAD_DIM)
    q = _rope((_head_rms(q.reshape(shp)) * qn.astype(F32)).astype(h.dtype), pos) * (ATT_HEAD_DIM ** -0.5)
    k = _rope((_head_rms(k.reshape(shp)) * kn.astype(F32)).astype(h.dtype), pos)
    return q, k, v.reshape(shp)


def _combine(qb, s_own, v_own, k_sel, v_sel, valid, n_sel):
    bn, nh, nq, _ = qb.shape
    s_sel = jnp.einsum('bhqd,bhqnkd->bhqnk', qb, k_sel).astype(F32)
    if valid is not None:
        s_sel = jnp.where(valid[..., None], s_sel, NEG)
    s_sel = s_sel.reshape(bn, nh, nq, n_sel * MOBA_BLOCK)
    p = jax.nn.softmax(jnp.concatenate([s_sel, s_own], axis=-1), axis=-1)
    p_sel = p[..., :n_sel * MOBA_BLOCK].reshape(bn, nh, nq, n_sel, MOBA_BLOCK)
    return (jnp.einsum('bhqnk,bhqnkd->bhqd', p_sel, v_sel.astype(F32))
            + jnp.einsum('bhqk,bhkd->bhqd', p[..., n_sel * MOBA_BLOCK:], v_own.astype(F32)))


def _moba_prompt(q, k, v):
    bn, s, nh, dh = q.shape
    n_full = s // MOBA_BLOCK
    n_sel = min(MOBA_TOPK, n_full)
    pad = -s % MOBA_BLOCK
    qt = q.transpose(0, 2, 1, 3)
    kt = jnp.pad(k, ((0, 0), (0, pad), (0, 0), (0, 0))).transpose(0, 2, 1, 3)
    vt = jnp.pad(v, ((0, 0), (0, pad), (0, 0), (0, 0))).transpose(0, 2, 1, 3)
    if n_sel > 0:
        kb = kt[:, :, :n_full * MOBA_BLOCK].reshape(bn, nh, n_full, MOBA_BLOCK, dh)
        vb = vt[:, :, :n_full * MOBA_BLOCK].reshape(bn, nh, n_full, MOBA_BLOCK, dh)
        k_mean = kb.astype(F32).mean(axis=3)
    bi = jnp.arange(bn)[:, None, None, None]
    hi = jnp.arange(nh)[None, :, None, None]

    def q_block(qb_i):
        q0 = qb_i * MOBA_Q_BLOCK
        qb = lax.dynamic_slice_in_dim(qt, q0, MOBA_Q_BLOCK, axis=2)
        qpos = q0 + jnp.arange(MOBA_Q_BLOCK)
        own = q0 // MOBA_BLOCK
        k_own = lax.dynamic_slice_in_dim(kt, own * MOBA_BLOCK, MOBA_BLOCK, axis=2)
        v_own = lax.dynamic_slice_in_dim(vt, own * MOBA_BLOCK, MOBA_BLOCK, axis=2)
        kpos = own * MOBA_BLOCK + jnp.arange(MOBA_BLOCK)
        s_own = jnp.einsum('bhqd,bhkd->bhqk', qb, k_own).astype(F32)
        s_own = jnp.where(kpos[None, :] <= qpos[:, None], s_own, NEG)
        if n_sel == 0:
            return jnp.einsum('bhqk,bhkd->bhqd', jax.nn.softmax(s_own, axis=-1), v_own.astype(F32))
        qblk = qpos // MOBA_BLOCK
        gate = jnp.einsum('bhqd,bhnd->bhqn', qb.astype(F32), k_mean)
        gate = jnp.where(jnp.arange(n_full)[None, :] < qblk[:, None], gate, NEG)
        _, sel = lax.top_k(gate, n_sel)
        valid = sel < qblk[:, None]
        return _combine(qb, s_own, v_own, kb[bi, hi, sel], vb[bi, hi, sel], valid, n_sel)

    o = lax.map(q_block, jnp.arange(s // MOBA_Q_BLOCK))
    return o.transpose(1, 0, 3, 2, 4).reshape(bn, s, nh * dh).astype(q.dtype)


def _moba_sample(q, k, v, ck, cv, page_table):
    bn, t, nh, dh = q.shape
    ppb = MOBA_BLOCK // PAGE_SIZE
    past = page_table.shape[1] * PAGE_SIZE
    n_full = past // MOBA_BLOCK
    n_sel = min(MOBA_TOPK, n_full)
    own_pages = (past - n_full * MOBA_BLOCK) // PAGE_SIZE
    qt = q.transpose(0, 2, 1, 3)
    k_own = k.transpose(0, 2, 1, 3)
    v_own = v.transpose(0, 2, 1, 3)
    qpos = past + jnp.arange(t)
    kpos = past + jnp.arange(t)
    if own_pages > 0:
        phys_own = page_table[:, n_full * ppb:n_full * ppb + own_pages]
        k_old = ck[phys_own].transpose(0, 2, 1, 3, 4).reshape(bn, nh, own_pages * PAGE_SIZE, dh)
        v_old = cv[phys_own].transpose(0, 2, 1, 3, 4).reshape(bn, nh, own_pages * PAGE_SIZE, dh)
        k_own = jnp.concatenate([k_old.astype(k_own.dtype), k_own], axis=2)
        v_own = jnp.concatenate([v_old.astype(v_own.dtype), v_own], axis=2)
        kpos = jnp.concatenate([n_full * MOBA_BLOCK + jnp.arange(own_pages * PAGE_SIZE), kpos])
    s_own = jnp.einsum('bhqd,bhkd->bhqk', qt, k_own).astype(F32)
    s_own = jnp.where(kpos[None, :] <= qpos[:, None], s_own, NEG)
    if n_sel == 0:
        o = jnp.einsum('bhqk,bhkd->bhqd', jax.nn.softmax(s_own, axis=-1), v_own.astype(F32))
    else:
        page_mean = ck.astype(F32).mean(axis=2)
        blk_mean = page_mean[page_table[:, :n_full * ppb]].reshape(bn, n_full, ppb, nh, dh).mean(axis=2)
        gate = jnp.einsum('bhqd,bnhd->bhqn', qt.astype(F32), blk_mean)
        _, sel = lax.top_k(gate, n_sel)
        lpage = sel[..., None] * ppb + jnp.arange(ppb)
        phys = page_table[jnp.arange(bn)[:, None, None, None, None], lpage]
        hidx = jnp.arange(nh)[None, :, None, None, None]
        k_sel = ck[phys, hidx].reshape(bn, nh, t, n_sel, MOBA_BLOCK, dh)
        v_sel = cv[phys, hidx].reshape(bn, nh, t, n_sel, MOBA_BLOCK, dh)
        o = _combine(qt, s_own, v_own, k_sel, v_sel, None, n_sel)
    return o.transpose(0, 2, 1, 3).reshape(bn, t, nh * dh).astype(q.dtype)


def _swiglu(h, wg, wu, wd):
    return (jax.nn.silu(h @ wg) * (h @ wu)) @ wd


def _moe(h, router, wg, wu, wd):
    logits = (h @ router).astype(F32)
    top_v, top_i = lax.top_k(logits, TOP_K)
    gates = jax.nn.softmax(top_v, axis=-1)
    combine = jnp.einsum('blk,blke->ble', gates, jax.nn.one_hot(top_i, N_EXPERTS, dtype=F32))
    out = jnp.zeros(h.shape, F32)
    for e in range(N_EXPERTS):
        out = out + combine[..., e:e + 1] * _swiglu(h, wg[e], wu[e], wd[e]).astype(F32)
    return out.astype(h.dtype)


def _ple(h, p, w_proj, w_gate, g):
    gate = jax.nn.sigmoid((_rmsnorm(h, g) @ w_gate).astype(F32))
    return h + (gate * (p @ w_proj).astype(F32)).astype(h.dtype)


def setup_inputs(seed: int = 0) -> dict:
    key = jax.random.key(seed)
    ks = iter(jax.random.split(key, 48))

    def nrm(shape, scale):
        return jax.random.normal(next(ks), shape, F32) * scale

    n_pages = PAST_LEN // PAGE_SIZE
    used = DEC_BATCH * n_pages
    n_pool = used + max(1, used // 4)
    x_prompt = nrm((BATCH, SEQ, D_MODEL), 1.0)
    x_sample = nrm((DEC_BATCH, DEC_SEQ, D_MODEL), 1.0)
    p_prompt = nrm((DEPTH, BATCH, SEQ, P_DIM), 1.0)
    p_sample = nrm((DEPTH, DEC_BATCH, DEC_SEQ, P_DIM), 1.0)
    state_ssm = nrm((N_HYB, DEC_BATCH, SSD_HEADS, SSD_STATE, SSD_HEAD_DIM), 0.5)
    state_conv = nrm((N_HYB, DEC_BATCH, SSD_CONV - 1, SSD_CONV_CH), 1.0)
    state_ret = nrm((N_HYB, DEC_BATCH, RET_HEADS, RET_DK, RET_DV), 1.0)
    cache_k = nrm((N_ATT, n_pool, ATT_HEADS, PAGE_SIZE, ATT_HEAD_DIM), 1.0)
    cache_v = nrm((N_ATT, n_pool, ATT_HEADS, PAGE_SIZE, ATT_HEAD_DIM), 1.0)
    page_table = jax.random.permutation(next(ks), n_pool)[:used].reshape(DEC_BATCH, n_pages).astype(jnp.int32)
    norm_mix = 1.0 + nrm((DEPTH, D_MODEL), 0.02)
    norm_ffn = 1.0 + nrm((DEPTH, D_MODEL), 0.02)
    norm_ple = 1.0 + nrm((DEPTH, D_MODEL), 0.02)
    ple_proj = nrm((DEPTH, P_DIM, D_MODEL), P_DIM ** -0.5)
    ple_gate = nrm((DEPTH, D_MODEL, D_MODEL), D_MODEL ** -0.5)
    hyb_w_in = nrm((N_HYB, D_MODEL, HYB_IN), D_MODEL ** -0.5)
    hyb_w_out = nrm((N_HYB, MIX_WIDTH, D_MODEL), MIX_WIDTH ** -0.5)
    ssd_conv_w = nrm((N_HYB, SSD_CONV, SSD_CONV_CH), SSD_CONV ** -0.5)
    ssd_conv_b = nrm((N_HYB, SSD_CONV_CH), 0.02)
    dt0 = jnp.exp(jax.random.uniform(next(ks), (N_HYB, SSD_HEADS)) * (math.log(0.1) - math.log(0.001)) + math.log(0.001))
    ssd_dt_bias = dt0 + jnp.log(-jnp.expm1(-dt0))
    ssd_a_log = jnp.log(jax.random.uniform(next(ks), (N_HYB, SSD_HEADS), minval=1.0, maxval=16.0))
    ssd_d = 1.0 + nrm((N_HYB, SSD_HEADS), 0.1)
    ssd_norm = 1.0 + nrm((N_HYB, SSD_D), 0.02)
    ret_norm = 1.0 + nrm((N_HYB, RET_V), 0.02)
    ffn_w_gate = nrm((N_HYB, D_MODEL, FFN_DIM), D_MODEL ** -0.5)
    ffn_w_up = nrm((N_HYB, D_MODEL, FFN_DIM), D_MODEL ** -0.5)
    ffn_w_down = nrm((N_HYB, FFN_DIM, D_MODEL), FFN_DIM ** -0.5)
    att_w_in = nrm((N_ATT, D_MODEL, 3 * ATT_D), D_MODEL ** -0.5)
    att_w_out = nrm((N_ATT, ATT_D, D_MODEL), ATT_D ** -0.5)
    att_q_norm = 1.0 + nrm((N_ATT, ATT_HEAD_DIM), 0.02)
    att_k_norm = 1.0 + nrm((N_ATT, ATT_HEAD_DIM), 0.02)
    moe_router = nrm((N_ATT, D_MODEL, N_EXPERTS), D_MODEL ** -0.5)
    moe_w_gate = nrm((N_ATT, N_EXPERTS, D_MODEL, EXPERT_DIM), D_MODEL ** -0.5)
    moe_w_up = nrm((N_ATT, N_EXPERTS, D_MODEL, EXPERT_DIM), D_MODEL ** -0.5)
    moe_w_down = nrm((N_ATT, N_EXPERTS, EXPERT_DIM, D_MODEL), EXPERT_DIM ** -0.5)
    return {'x_prompt': x_prompt, 'x_sample': x_sample, 'p_prompt': p_prompt, 'p_sample': p_sample,
            'state_ssm': state_ssm, 'state_conv': state_conv, 'state_ret': state_ret,
            'cache_k': cache_k, 'cache_v': cache_v, 'page_table': page_table,
            'norm_mix': norm_mix, 'norm_ffn': norm_ffn, 'norm_ple': norm_ple,
            'ple_proj': ple_proj, 'ple_gate': ple_gate,
            'hyb_w_in': hyb_w_in, 'hyb_w_out': hyb_w_out, 'ssd_conv_w': ssd_conv_w, 'ssd_conv_b': ssd_conv_b,
            'ssd_dt_bias': ssd_dt_bias, 'ssd_a_log': ssd_a_log, 'ssd_d': ssd_d, 'ssd_norm': ssd_norm,
            'ret_norm': ret_norm, 'ffn_w_gate': ffn_w_gate, 'ffn_w_up': ffn_w_up, 'ffn_w_down': ffn_w_down,
            'att_w_in': att_w_in, 'att_w_out': att_w_out, 'att_q_norm': att_q_norm, 'att_k_norm': att_k_norm,
            'moe_router': moe_router, 'moe_w_gate': moe_w_gate, 'moe_w_up': moe_w_up, 'moe_w_down': moe_w_down}


def reference(x_prompt, x_sample, p_prompt, p_sample, state_ssm, state_conv, state_ret, cache_k, cache_v, page_table,
              norm_mix, norm_ffn, norm_ple, ple_proj, ple_gate, hyb_w_in, hyb_w_out, ssd_conv_w, ssd_conv_b,
              ssd_dt_bias, ssd_a_log, ssd_d, ssd_norm, ret_norm, ffn_w_gate, ffn_w_up, ffn_w_down,
              att_w_in, att_w_out, att_q_norm, att_k_norm, moe_router, moe_w_gate, moe_w_up, moe_w_down):
    bp, sp, _ = x_prompt.shape
    ts = x_sample.shape[1]
    past_len = page_table.shape[1] * PAGE_SIZE
    pos_p = jnp.arange(sp)
    pos_s = past_len + jnp.arange(ts)
    hp, hs = x_prompt, x_sample
    ssm_p, conv_p, ret_p, ssm_s, conv_s, ret_s = [], [], [], [], [], []
    k_p, v_p, k_s, v_s = [], [], [], []
    for i in range(DEPTH):
        j = i // 2
        if i % 2 == 0:
            w = (hyb_w_in[j], hyb_w_out[j], ssd_conv_w[j], ssd_conv_b[j], ssd_dt_bias[j], ssd_a_log[j],
                 ssd_d[j], ssd_norm[j], ret_norm[j])
            z_conv = jnp.zeros((bp, SSD_CONV - 1, SSD_CONV_CH), hp.dtype)
            z_ssm = jnp.zeros((bp, SSD_HEADS, SSD_STATE, SSD_HEAD_DIM), F32)
            z_ret = jnp.zeros((bp, RET_HEADS, RET_DK, RET_DV), F32)
            m, a_, c_, r_ = _hybrid_mixer(_rmsnorm(hp, norm_mix[i]), pos_p, z_conv, z_ssm, z_ret, *w)
            hp = hp + m
            ssm_p.append(a_); conv_p.append(c_); ret_p.append(r_)
            m, a_, c_, r_ = _hybrid_mixer(_rmsnorm(hs, norm_mix[i]), pos_s, state_conv[j], state_ssm[j], state_ret[j], *w)
            hs = hs + m
            ssm_s.append(a_); conv_s.append(c_); ret_s.append(r_)
            hp = hp + _swiglu(_rmsnorm(hp, norm_ffn[i]), ffn_w_gate[j], ffn_w_up[j], ffn_w_down[j])
            hs = hs + _swiglu(_rmsnorm(hs, norm_ffn[i]), ffn_w_gate[j], ffn_w_up[j], ffn_w_down[j])
        else:
            q, k, v = _att_qkv(_rmsnorm(hp, norm_mix[i]), pos_p, att_w_in[j], att_q_norm[j], att_k_norm[j])
            hp = hp + _moba_prompt(q, k, v) @ att_w_out[j]
            k_p.append(k.transpose(0, 2, 1, 3)); v_p.append(v.transpose(0, 2, 1, 3))
            q, k, v = _att_qkv(_rmsnorm(hs, norm_mix[i]), pos_s, att_w_in[j], att_q_norm[j], att_k_norm[j])
            hs = hs + _moba_sample(q, k, v, cache_k[j], cache_v[j], page_table) @ att_w_out[j]
            k_s.append(k.transpose(0, 2, 1, 3)); v_s.append(v.transpose(0, 2, 1, 3))
            hp = hp + _moe(_rmsnorm(hp, norm_ffn[i]), moe_router[j], moe_w_gate[j], moe_w_up[j], moe_w_down[j])
            hs = hs + _moe(_rmsnorm(hs, norm_ffn[i]), moe_router[j], moe_w_gate[j], moe_w_up[j], moe_w_down[j])
        hp = _ple(hp, p_prompt[i], ple_proj[i], ple_gate[i], norm_ple[i])
        hs = _ple(hs, p_sample[i], ple_proj[i], ple_gate[i], norm_ple[i])
    return (hp, hs, jnp.stack(ssm_p), jnp.stack(conv_p), jnp.stack(ret_p), jnp.stack(k_p), jnp.stack(v_p),
            jnp.stack(ssm_s), jnp.stack(conv_s), jnp.stack(ret_s), jnp.stack(k_s), jnp.stack(v_s))
```

```python
import functools
import math

import jax
import jax.numpy as jnp
from jax import lax
from jax.experimental import pallas as pl
from jax.experimental.pallas import tpu as pltpu

F32 = jnp.float32
BF16 = jnp.bfloat16
HIGHEST = lax.Precision.HIGHEST

D_MODEL = 1024
P_DIM = 256
EPS = 1e-6
ROPE_THETA = 10000.0
PAGE_SIZE = 128
SSD_HEADS = 16
SSD_HEAD_DIM = 64
SSD_D = SSD_HEADS * SSD_HEAD_DIM
SSD_GROUPS = 2
SSD_STATE = 128
SSD_CONV = 4
SSD_CONV_CH = SSD_D + 2 * SSD_GROUPS * SSD_STATE
RET_HEADS = 8
RET_DK = 64
RET_DV = 128
RET_QK = RET_HEADS * RET_DK
RET_V = RET_HEADS * RET_DV
MIX_WIDTH = SSD_D + RET_V
SCAN_CHUNK = 128
ATT_HEADS = 16
ATT_HEAD_DIM = 64
ATT_D = ATT_HEADS * ATT_HEAD_DIM
MOBA_BLOCK = 256
MOBA_TOPK = 3
N_EXPERTS = 8
NEG = -1e30

LANES = 128
SUBLANES = 8
SAMPLE_ROWS = 8
DT_PAD = LANES
U_Z, U_RV, U_RG, U_XBC, U_RQ, U_RK, U_DT = 0, 1024, 2048, 3072, 4608, 5120, 5632
U_WIDTH = U_DT + DT_PAD
VMEM_LIMIT = 56 * 1024 * 1024


def _cparams(sem):
    return pltpu.CompilerParams(dimension_semantics=sem, vmem_limit_bytes=VMEM_LIMIT)


def _rms(x, g):
    return x * lax.rsqrt(jnp.mean(x * x, axis=-1, keepdims=True) + EPS) * g


def _silu(x):
    return x * jax.nn.sigmoid(x)


NN = (((1,), (0,)), ((), ()))
NT = (((1,), (1,)), ((), ()))
TN = (((0,), (0,)), ((), ()))


def _mm(a, b, mxu, dims=NN):
    return lax.dot_general(a.astype(mxu), b.astype(mxu), dims, preferred_element_type=F32,
                           precision=HIGHEST if mxu == F32 else None)


def _norm_matmul_body(x_ref, g_ref, w_ref, o_ref, xn_ref):
    @pl.when(pl.program_id(1) == 0)
    def _():
        xn_ref[...] = _rms(x_ref[...], g_ref[...]).astype(xn_ref.dtype)

    o_ref[...] = _mm(xn_ref[...], w_ref[...], w_ref.dtype)


def _norm_matmul(x, g, w, tm, tn):
    m, d = x.shape
    n = w.shape[1]
    return pl.pallas_call(
        _norm_matmul_body,
        grid=(m // tm, n // tn),
        in_specs=[pl.BlockSpec((tm, d), lambda i, j: (i, 0)),
                  pl.BlockSpec((1, d), lambda i, j: (0, 0)),
                  pl.BlockSpec((d, tn), lambda i, j: (0, j))],
        out_specs=pl.BlockSpec((tm, tn), lambda i, j: (i, j)),
        out_shape=jax.ShapeDtypeStruct((m, n), F32),
        scratch_shapes=[pltpu.VMEM((tm, d), w.dtype)],
        compiler_params=_cparams(("parallel", "arbitrary")),
        name="norm_matmul",
    )(x, g.reshape(1, d), w)


def _matmul_res_body(a_ref, w_ref, r_ref, o_ref):
    o_ref[...] = r_ref[...] + _mm(a_ref[...], w_ref[...], w_ref.dtype)


def _matmul_res(a, w, res, tm):
    m, k = a.shape
    n = w.shape[1]
    return pl.pallas_call(
        _matmul_res_body,
        grid=(m // tm,),
        in_specs=[pl.BlockSpec((tm, k), lambda i: (i, 0)),
                  pl.BlockSpec((k, n), lambda i: (0, 0)),
                  pl.BlockSpec((tm, n), lambda i: (i, 0))],
        out_specs=pl.BlockSpec((tm, n), lambda i: (i, 0)),
        out_shape=jax.ShapeDtypeStruct((m, n), F32),
        compiler_params=_cparams(("parallel",)),
        name="matmul_res",
    )(a, w, res)


def _rope_lanes(x, cos, sin_signed):
    lane = lax.broadcasted_iota(jnp.int32, x.shape, 1)
    rot = jnp.where(lane % ATT_HEAD_DIM < ATT_HEAD_DIM // 2,
                    pltpu.roll(x, LANES - ATT_HEAD_DIM // 2, 1),
                    pltpu.roll(x, ATT_HEAD_DIM // 2, 1))
    return x * cos + rot * sin_signed


def _mixer_body(z_ref, rv_ref, rg_ref, xbc_ref, rq_ref, rk_ref, dt_ref, cos_ref, sin_ref,
                conv0_ref, ssm0_ref, ret0_ref, convw_ref, convb_ref, dtb_ref, alog_ref,
                dskip_ref, ssdg_ref, retg_ref,
                mix_ref, ssm_ref, conv_ref, ret_ref, *, q_len, n_valid, mxu):
    c = pl.program_id(1)

    @pl.when(c == 0)
    def _():
        ssm_ref[...] = ssm0_ref[...]
        ret_ref[...] = ret0_ref[...]
        conv_ref[...] = conv0_ref[...]

    row = lax.broadcasted_iota(jnp.int32, (q_len, 1), 0)
    valid = row < n_valid
    qi = lax.broadcasted_iota(jnp.int32, (q_len, q_len), 0)
    ki = lax.broadcasted_iota(jnp.int32, (q_len, q_len), 1)
    causal = ki <= qi

    xbc = xbc_ref[...]
    ext = jnp.concatenate([conv_ref[0], xbc], axis=0)
    acc = convb_ref[...]
    for j in range(SSD_CONV):
        lo = SUBLANES - (SSD_CONV - 1) + j
        acc = acc + ext[lo:lo + q_len] * convw_ref[j:j + 1, :]
    conv_ref[0] = ext[n_valid:n_valid + SUBLANES]
    xc = _silu(acc)
    xs = xc[:, :SSD_D]
    bs = xc[:, SSD_D:SSD_D + SSD_GROUPS * SSD_STATE]
    cs = xc[:, SSD_D + SSD_GROUPS * SSD_STATE:]

    dt_in = dt_ref[...] + dtb_ref[...]
    dt = jnp.maximum(dt_in, 0.0) + jnp.log1p(jnp.exp(-jnp.abs(dt_in)))
    dt = jnp.where(valid, dt, 0.0)
    la = dt * (-jnp.exp(alog_ref[...]))
    cum = jnp.dot(causal.astype(F32), la, preferred_element_type=F32, precision=HIGHEST)
    cum_t = cum.T
    dt_t = dt.T
    last = cum[q_len - 1:q_len, :]
    e_cum = jnp.exp(cum)
    w_all = jnp.exp(last - cum) * dt
    e_last = jnp.exp(last)
    rep = SSD_HEADS // SSD_GROUPS
    ys = []
    for g in range(SSD_GROUPS):
        c_g = cs[:, g * SSD_STATE:(g + 1) * SSD_STATE].astype(mxu)
        b_g = bs[:, g * SSD_STATE:(g + 1) * SSD_STATE].astype(mxu)
        cb = _mm(c_g, b_g, mxu, NT)
        for h in range(g * rep, (g + 1) * rep):
            diff = cum[:, h:h + 1] - cum_t[h:h + 1, :]
            m = cb * jnp.exp(jnp.where(causal, diff, -jnp.inf)) * dt_t[h:h + 1, :]
            x_h = xs[:, h * SSD_HEAD_DIM:(h + 1) * SSD_HEAD_DIM]
            s_in = ssm_ref[0, h]
            y_h = _mm(m, x_h, mxu) + e_cum[:, h:h + 1] * _mm(c_g, s_in, mxu)
            ssm_ref[0, h] = e_last[:, h:h + 1] * s_in + _mm(b_g, x_h * w_all[:, h:h + 1], mxu, TN)
            ys.append(y_h)
    y = jnp.concatenate(ys, axis=1) + dskip_ref[...] * xs
    y_ssd = _rms(y * _silu(z_ref[...]), ssdg_ref[...])

    cos = cos_ref[...]
    sin = sin_ref[...]
    cnt = jnp.minimum(row + 1, n_valid).astype(F32)
    cnt_t = jnp.minimum(ki[0:1, :] + 1, n_valid).astype(F32)
    yrs = []
    for hp in range(RET_HEADS // 2):
        sl = slice(hp * LANES, (hp + 1) * LANES)
        q2 = _rope_lanes(rq_ref[:, sl], cos, sin)
        k2 = _rope_lanes(rk_ref[:, sl], cos, sin) * (RET_DK ** -0.5)
        for h in (2 * hp, 2 * hp + 1):
            log_g = math.log1p(-(2.0 ** (-5.0 - h)))
            o = (h % 2) * RET_DK
            q_h = q2[:, o:o + RET_DK]
            k_h = k2[:, o:o + RET_DK]
            v_h = rv_ref[:, h * RET_DV:(h + 1) * RET_DV]
            dec = jnp.exp(jnp.where(causal, (cnt - cnt_t) * log_g, -jnp.inf))
            s_in = ret_ref[0, h]
            y_h = _mm(_mm(q_h, k_h, mxu, NT) * dec, v_h, mxu) + jnp.exp(cnt * log_g) * _mm(q_h, s_in, mxu)
            w_k = jnp.where(valid, jnp.exp((n_valid - cnt) * log_g), 0.0)
            ret_ref[0, h] = math.exp(n_valid * log_g) * s_in + _mm(k_h, v_h * w_k, mxu, TN)
            yrs.append(y_h * lax.rsqrt(jnp.mean(y_h * y_h, axis=-1, keepdims=True) + EPS))
    yr = jnp.concatenate(yrs, axis=1) * retg_ref[...] * _silu(rg_ref[...])
    mix_ref[...] = jnp.concatenate([y_ssd, yr], axis=1).astype(mix_ref.dtype)


def _mixer(u, cos, sin, conv0, ssm0, ret0, conv_w, conv_b, dt_bias, a_log, d_skip, ssd_g, ret_g,
           *, n_seq, n_chunks, q_len, n_valid, pos_per_chunk, mxu):
    def ub(width, off):
        return pl.BlockSpec((q_len, width), lambda b, c: (b * n_chunks + c, off // width))

    def full2(a):
        return pl.BlockSpec(a.shape, lambda b, c: (0, 0))

    pos_spec = pl.BlockSpec((q_len, LANES), (lambda b, c: (c, 0)) if pos_per_chunk else (lambda b, c: (0, 0)))
    state = lambda a: pl.BlockSpec((1,) + a.shape[1:], lambda b, c: (b,) + (0,) * (a.ndim - 1))
    pad16 = lambda v: jnp.pad(v.reshape(1, -1), ((0, 0), (0, DT_PAD - SSD_HEADS)))
    params = [conv_w, conv_b.reshape(1, -1), pad16(dt_bias), pad16(a_log),
              jnp.repeat(d_skip, SSD_HEAD_DIM).reshape(1, -1), ssd_g.reshape(1, -1), ret_g.reshape(1, -1)]
    rows = n_seq * n_chunks * q_len
    out_shape = (jax.ShapeDtypeStruct((rows, MIX_WIDTH), mxu),
                 jax.ShapeDtypeStruct(ssm0.shape, F32),
                 jax.ShapeDtypeStruct(conv0.shape, F32),
                 jax.ShapeDtypeStruct(ret0.shape, F32))
    return pl.pallas_call(
        functools.partial(_mixer_body, q_len=q_len, n_valid=n_valid, mxu=mxu),
        grid=(n_seq, n_chunks),
        in_specs=[ub(SSD_D, U_Z), ub(RET_V, U_RV), ub(RET_V, U_RG), ub(SSD_CONV_CH, U_XBC),
                  ub(RET_QK, U_RQ), ub(RET_QK, U_RK), ub(DT_PAD, U_DT), pos_spec, pos_spec,
                  state(conv0), state(ssm0), state(ret0)] + [full2(p) for p in params],
        out_specs=(pl.BlockSpec((q_len, MIX_WIDTH), lambda b, c: (b * n_chunks + c, 0)),
                   state(ssm0), state(conv0), state(ret0)),
        out_shape=out_shape,
        compiler_params=_cparams(("parallel", "arbitrary")),
        name="mixer",
    )(u, u, u, u, u, u, u, cos, sin, conv0, ssm0, ret0, *params)


def _route(xn, router):
    logits = jnp.dot(xn, router, preferred_element_type=F32, precision=HIGHEST)
    lane = lax.broadcasted_iota(jnp.int32, logits.shape, 1)
    lg = jnp.where(lane < N_EXPERTS, logits, -jnp.inf)
    m1 = jnp.max(lg, axis=1, keepdims=True)
    i1 = jnp.min(jnp.where(lg == m1, lane, LANES), axis=1, keepdims=True)
    lg2 = jnp.where(lane == i1, -jnp.inf, lg)
    m2 = jnp.max(lg2, axis=1, keepdims=True)
    i2 = jnp.min(jnp.where(lg2 == m2, lane, LANES), axis=1, keepdims=True)
    e2 = jnp.exp(m2 - m1)
    den = 1.0 + e2
    return jnp.where(lane == i1, 1.0 / den, 0.0) + jnp.where(lane == i2, e2 / den, 0.0)


def _ffn_body(*refs, routed):
    if routed:
        x_ref, g_ref, r_ref, wg_ref, wu_ref, wd_ref, o_ref, xn_ref, acc_ref, comb_ref = refs
    else:
        x_ref, g_ref, wg_ref, wu_ref, wd_ref, o_ref, xn_ref, acc_ref = refs
    e = pl.program_id(1)
    f = pl.program_id(2)

    @pl.when((e == 0) & (f == 0))
    def _():
        xn = _rms(x_ref[...], g_ref[...])
        xn_ref[...] = xn.astype(xn_ref.dtype)
        acc_ref[...] = jnp.zeros_like(acc_ref)
        if routed:
            comb_ref[...] = _route(xn, r_ref[...])

    xn = xn_ref[...]
    mxu = wg_ref.dtype
    a = _silu(_mm(xn, wg_ref[0], mxu)) * _mm(xn, wu_ref[0], mxu)
    y = _mm(a, wd_ref[0], mxu)
    if routed:
        lane = lax.broadcasted_iota(jnp.int32, comb_ref.shape, 1)
        y = y * jnp.sum(jnp.where(lane == e, comb_ref[...], 0.0), axis=1, keepdims=True)
    acc_ref[...] += y

    @pl.when((e == pl.num_programs(1) - 1) & (f == pl.num_programs(2) - 1))
    def _():
        o_ref[...] = x_ref[...] + acc_ref[...]


def _ffn(x, g, wg, wu, wd, router, tm, tf):
    m, d = x.shape
    n_e, _, f_dim = wg.shape
    routed = router is not None
    in_specs = [pl.BlockSpec((tm, d), lambda i, e, f: (i, 0)),
                pl.BlockSpec((1, d), lambda i, e, f: (0, 0))]
    args = [x, g.reshape(1, d)]
    scratch = [pltpu.VMEM((tm, d), wg.dtype), pltpu.VMEM((tm, d), F32)]
    if routed:
        in_specs.append(pl.BlockSpec((d, LANES), lambda i, e, f: (0, 0)))
        args.append(jnp.pad(router, ((0, 0), (0, LANES - n_e))))
        scratch.append(pltpu.VMEM((tm, LANES), F32))
    in_specs += [pl.BlockSpec((1, d, tf), lambda i, e, f: (e, 0, f)),
                 pl.BlockSpec((1, d, tf), lambda i, e, f: (e, 0, f)),
                 pl.BlockSpec((1, tf, d), lambda i, e, f: (e, f, 0))]
    return pl.pallas_call(
        functools.partial(_ffn_body, routed=routed),
        grid=(m // tm, n_e, f_dim // tf),
        in_specs=in_specs,
        out_specs=pl.BlockSpec((tm, d), lambda i, e, f: (i, 0)),
        out_shape=jax.ShapeDtypeStruct((m, d), F32),
        scratch_shapes=scratch,
        compiler_params=_cparams(("parallel", "arbitrary", "arbitrary")),
        name="moe" if routed else "ffn",
    )(*args, wg, wu, wd)


def _ple_body(x_ref, p_ref, g_ref, wg_ref, wp_ref, o_ref):
    x = x_ref[...]
    mxu = wg_ref.dtype
    gate = jax.nn.sigmoid(_mm(_rms(x, g_ref[...]), wg_ref[...], mxu))
    o_ref[...] = x + gate * _mm(p_ref[...], wp_ref[...], mxu)


def _ple(x, p_all, layer, g, w_gate, w_proj, tm):
    m, d = x.shape
    return pl.pallas_call(
        _ple_body,
        grid=(m // tm,),
        in_specs=[pl.BlockSpec((tm, d), lambda i: (i, 0)),
                  pl.BlockSpec((None, tm, P_DIM), lambda i: (layer, i, 0)),
                  pl.BlockSpec((1, d), lambda i: (0, 0)),
                  pl.BlockSpec((d, d), lambda i: (0, 0)),
                  pl.BlockSpec((P_DIM, d), lambda i: (0, 0))],
        out_specs=pl.BlockSpec((tm, d), lambda i: (i, 0)),
        out_shape=jax.ShapeDtypeStruct((m, d), F32),
        compiler_params=_cparams(("parallel",)),
        name="ple",
    )(x, p_all, g.reshape(1, d), w_gate, w_proj)


def _qkv_post_body(qkv_ref, qn_ref, kn_ref, cos_ref, sin_ref, q_ref, k_ref, v_ref):
    half = ATT_HEAD_DIM // 2
    cos = cos_ref[:, :ATT_HEAD_DIM]
    sin = sin_ref[:, :ATT_HEAD_DIM]

    def norm_rope(x, gain):
        y = x * lax.rsqrt(jnp.mean(x * x, axis=-1, keepdims=True) + EPS) * gain
        rot = jnp.concatenate([y[:, half:], y[:, :half]], axis=1)
        return y * cos + rot * sin

    for h in range(ATT_HEADS):
        sl = slice(h * ATT_HEAD_DIM, (h + 1) * ATT_HEAD_DIM)
        q_ref[0, h] = norm_rope(qkv_ref[:, sl], qn_ref[...]) * (ATT_HEAD_DIM ** -0.5)
        k_ref[0, h] = norm_rope(qkv_ref[:, ATT_D + h * ATT_HEAD_DIM:ATT_D + (h + 1) * ATT_HEAD_DIM],
                                kn_ref[...])
        v_ref[0, h] = qkv_ref[:, 2 * ATT_D + h * ATT_HEAD_DIM:2 * ATT_D + (h + 1) * ATT_HEAD_DIM]


def _qkv_post(qkv, qn, kn, cos, sin, n_seq, seq, ts, pos_per_tile):
    nt = seq // ts
    hm = jax.ShapeDtypeStruct((n_seq, ATT_HEADS, seq, ATT_HEAD_DIM), F32)
    hm_spec = pl.BlockSpec((1, ATT_HEADS, ts, ATT_HEAD_DIM), lambda b, t: (b, 0, t, 0))
    pos_spec = pl.BlockSpec((ts, LANES), (lambda b, t: (t, 0)) if pos_per_tile else (lambda b, t: (0, 0)))
    vec = pl.BlockSpec((1, ATT_HEAD_DIM), lambda b, t: (0, 0))
    return pl.pallas_call(
        _qkv_post_body,
        grid=(n_seq, nt),
        in_specs=[pl.BlockSpec((ts, 3 * ATT_D), lambda b, t: (b * nt + t, 0)), vec, vec, pos_spec, pos_spec],
        out_specs=(hm_spec, hm_spec, hm_spec),
        out_shape=(hm, hm, hm),
        compiler_params=_cparams(("parallel", "parallel")),
        name="qkv_post",
    )(qkv, qn.reshape(1, -1), kn.reshape(1, -1), cos, sin)


HEADS_PER_STEP = LANES // ATT_HEAD_DIM


def _moba_prompt_body(q_ref, k_ref, v_ref, o_ref, kb_ref, vb_ref, km_ref, *, n_blocks):
    qi = pl.program_id(2)
    blk = MOBA_BLOCK

    @pl.when(qi == 0)
    def _():
        km_ref[...] = jnp.zeros_like(km_ref)
        for hh in range(HEADS_PER_STEP):
            kb_ref[hh] = k_ref[0, hh].astype(BF16)
            vb_ref[hh] = v_ref[0, hh].astype(BF16)
            for n in range(n_blocks):
                km_ref[hh, n:n + 1, :] = jnp.mean(k_ref[0, hh, n * blk:(n + 1) * blk, :], axis=0, keepdims=True)

    lane = lax.broadcasted_iota(jnp.int32, (blk, LANES), 1)
    r_i = lax.broadcasted_iota(jnp.int32, (blk, blk), 0)
    c_i = lax.broadcasted_iota(jnp.int32, (blk, blk), 1)
    outs = []
    for hh in range(HEADS_PER_STEP):
        q = q_ref[0, hh]
        qb = q.astype(BF16)
        gate = lax.dot_general(q, km_ref[hh], (((1,), (1,)), ((), ())),
                               preferred_element_type=F32, precision=HIGHEST)
        g = jnp.where(lane < qi, gate, NEG)
        sel = jnp.zeros((blk, LANES), F32)
        for _ in range(MOBA_TOPK):
            mx = jnp.max(g, axis=1, keepdims=True)
            idx = jnp.min(jnp.where(g == mx, lane, LANES), axis=1, keepdims=True)
            pick = lane == idx
            sel = jnp.where(pick & (lane < qi), 1.0, sel)
            g = jnp.where(pick, -jnp.inf, g)

        own = pl.multiple_of(qi * blk, blk)
        s = _mm(qb, kb_ref[hh, pl.ds(own, blk), :], BF16, NT)
        s = jnp.where(c_i <= r_i, s, NEG)
        m0 = jnp.max(s, axis=1, keepdims=True)
        p = jnp.exp(s - m0)
        l0 = jnp.sum(p, axis=1, keepdims=True)
        a0 = _mm(p, vb_ref[hh, pl.ds(own, blk), :], BF16)

        def body(n, carry):
            m_i, l_i, acc = carry
            off = pl.multiple_of(n * blk, blk)
            s = _mm(qb, kb_ref[hh, pl.ds(off, blk), :], BF16, NT)
            picked = jnp.max(jnp.where(lane == n, sel, 0.0), axis=1, keepdims=True) > 0.0
            s = jnp.where(picked, s, NEG)
            m_new = jnp.maximum(m_i, jnp.max(s, axis=1, keepdims=True))
            alpha = jnp.exp(m_i - m_new)
            p = jnp.exp(s - m_new)
            l_new = alpha * l_i + jnp.sum(p, axis=1, keepdims=True)
            acc = alpha * acc + _mm(p, vb_ref[hh, pl.ds(off, blk), :], BF16)
            return m_new, l_new, acc

        _, l_f, acc = lax.fori_loop(0, qi, body, (m0, l0, a0))
        outs.append(acc / l_f)
    o_ref[0] = jnp.concatenate(outs, axis=1).astype(BF16)


def _moba_prompt(q, k, v):
    bn, nh, s, dh = q.shape
    n_blocks = s // MOBA_BLOCK
    hps = HEADS_PER_STEP
    return pl.pallas_call(
        functools.partial(_moba_prompt_body, n_blocks=n_blocks),
        grid=(bn, nh // hps, n_blocks),
        in_specs=[pl.BlockSpec((1, hps, MOBA_BLOCK, dh), lambda b, h, i: (b, h, i, 0)),
                  pl.BlockSpec((1, hps, s, dh), lambda b, h, i: (b, h, 0, 0)),
                  pl.BlockSpec((1, hps, s, dh), lambda b, h, i: (b, h, 0, 0))],
        out_specs=pl.BlockSpec((1, MOBA_BLOCK, hps * dh), lambda b, h, i: (b, i, h)),
        out_shape=jax.ShapeDtypeStruct((bn, s, nh * dh), BF16),
        scratch_shapes=[pltpu.VMEM((hps, s, dh), BF16), pltpu.VMEM((hps, s, dh), BF16),
                        pltpu.VMEM((hps, LANES, dh), F32)],
        compiler_params=_cparams(("parallel", "parallel", "arbitrary")),
        name="moba_prompt",
    )(q, k, v)


def _page_mean_body(k_ref, o_ref):
    o_ref[0] = jnp.concatenate(
        [jnp.mean(k_ref[0, :, h], axis=1) for h in range(ATT_HEADS)], axis=1)


def _page_mean(cache_k, pages_per_step):
    nl, n_pool, nh, ps, dh = cache_k.shape
    assert n_pool % pages_per_step == 0
    return pl.pallas_call(
        _page_mean_body,
        grid=(nl, n_pool // pages_per_step),
        in_specs=[pl.BlockSpec((1, pages_per_step, nh, ps, dh), lambda l, i: (l, i, 0, 0, 0))],
        out_specs=pl.BlockSpec((1, pages_per_step, nh * dh), lambda l, i: (l, i, 0)),
        out_shape=jax.ShapeDtypeStruct((nl, n_pool, nh * dh), F32),
        compiler_params=_cparams(("parallel", "parallel")),
        name="page_mean",
    )(cache_k)


def _sample_select_body(pt_ref, q_ref, pm_ref, sel_ref, gath_ref, *, n_pages):
    b = pl.program_id(0)
    n_blk = n_pages // 2
    for r in range(n_pages):
        logical = 2 * r if r < n_blk else 2 * (r - n_blk) + 1
        page = pt_ref[b * n_pages + logical]
        gath_ref[r:r + 1, :] = pm_ref[0, pl.ds(page, 1), :]
    lane = lax.broadcasted_iota(jnp.int32, (SAMPLE_ROWS, LANES), 1)
    for h in range(ATT_HEADS):
        pm_h = gath_ref[:, h * ATT_HEAD_DIM:(h + 1) * ATT_HEAD_DIM]
        gp = lax.dot_general(q_ref[0, h], pm_h, (((1,), (1,)), ((), ())),
                             preferred_element_type=F32, precision=HIGHEST)
        gate = 0.5 * (gp[:, :n_blk] + gp[:, n_blk:])
        g = jnp.concatenate([gate, jnp.full((SAMPLE_ROWS, LANES - n_blk), -jnp.inf, F32)], axis=1)
        out = jnp.zeros((SAMPLE_ROWS, LANES), jnp.int32)
        for j in range(MOBA_TOPK):
            mx = jnp.max(g, axis=1, keepdims=True)
            idx = jnp.min(jnp.where(g == mx, lane, LANES), axis=1, keepdims=True)
            out = jnp.where(lane == j, idx, out)
            g = jnp.where(lane == idx, -jnp.inf, g)
        sel_ref[0, h] = out


def _sample_select(page_table_flat, q, page_means, layer, n_pages):
    bn, nh, rows, dh = q.shape
    n_pool = page_means.shape[1]
    return pl.pallas_call(
        functools.partial(_sample_select_body, n_pages=n_pages),
        grid_spec=pltpu.PrefetchScalarGridSpec(
            num_scalar_prefetch=1,
            grid=(bn,),
            in_specs=[pl.BlockSpec((1, nh, rows, dh), lambda b, pt: (b, 0, 0, 0)),
                      pl.BlockSpec((1, n_pool, nh * dh), lambda b, pt: (layer, 0, 0))],
            out_specs=pl.BlockSpec((1, nh, rows, LANES), lambda b, pt: (b, 0, 0, 0)),
            scratch_shapes=[pltpu.VMEM((n_pages, nh * dh), F32)]),
        out_shape=jax.ShapeDtypeStruct((bn, nh, rows, LANES), jnp.int32),
        compiler_params=_cparams(("arbitrary",)),
        name="sample_select",
    )(page_table_flat, q, page_means)


def _sample_attn_body(pt_ref, sel_ref, q_ref, ko_ref, vo_ref, *rest, n_tok, n_sel_pages):
    k_pages = rest[:n_tok * n_sel_pages]
    v_pages = rest[n_tok * n_sel_pages:2 * n_tok * n_sel_pages]
    o_ref = rest[2 * n_tok * n_sel_pages]
    q = q_ref[0, 0]
    r_i = lax.broadcasted_iota(jnp.int32, (SAMPLE_ROWS, SAMPLE_ROWS), 0)
    c_i = lax.broadcasted_iota(jnp.int32, (SAMPLE_ROWS, SAMPLE_ROWS), 1)
    s_own = jnp.where(c_i <= r_i, _mm(q, ko_ref[0, 0], F32, NT), NEG)
    v_own = vo_ref[0, 0]
    rows = []
    for t in range(n_tok):
        k_t = jnp.concatenate([k_pages[t * n_sel_pages + i][0, 0, 0] for i in range(n_sel_pages)], axis=0)
        v_t = jnp.concatenate([v_pages[t * n_sel_pages + i][0, 0, 0] for i in range(n_sel_pages)], axis=0)
        s_sel = _mm(q[t:t + 1], k_t, F32, NT)
        s_o = s_own[t:t + 1]
        mx = jnp.maximum(jnp.max(s_sel, axis=1, keepdims=True), jnp.max(s_o, axis=1, keepdims=True))
        p_sel = jnp.exp(s_sel - mx)
        p_o = jnp.exp(s_o - mx)
        den = jnp.sum(p_sel, axis=1, keepdims=True) + jnp.sum(p_o, axis=1, keepdims=True)
        rows.append((_mm(p_sel, v_t, F32) + _mm(p_o, v_own, F32)) / den)
    rows.append(jnp.zeros((SAMPLE_ROWS - n_tok, ATT_HEAD_DIM), F32))
    o_ref[0, 0] = jnp.concatenate(rows, axis=0)


def _sample_attn(page_table_flat, sel_flat, q, k_own, v_own, cache_k, cache_v, layer, n_pages, n_tok):
    bn, nh, rows, dh = q.shape
    ppb = MOBA_BLOCK // PAGE_SIZE
    n_sel_pages = MOBA_TOPK * ppb

    def page_spec(t, i):
        def imap(b, h, pt, sel):
            blk = sel[((b * nh + h) * n_tok + t) * MOBA_TOPK + i // ppb]
            return (layer, pt[b * n_pages + blk * ppb + i % ppb], h, 0, 0)
        return pl.BlockSpec((1, 1, 1, PAGE_SIZE, dh), imap)

    own = pl.BlockSpec((1, 1, rows, dh), lambda b, h, pt, sel: (b, h, 0, 0))
    pages = [page_spec(t, i) for t in range(n_tok) for i in range(n_sel_pages)]
    return pl.pallas_call(
        functools.partial(_sample_attn_body, n_tok=n_tok, n_sel_pages=n_sel_pages),
        grid_spec=pltpu.PrefetchScalarGridSpec(
            num_scalar_prefetch=2,
            grid=(bn, nh),
            in_specs=[own, own, own] + pages + pages,
            out_specs=own),
        out_shape=jax.ShapeDtypeStruct((bn, nh, rows, dh), F32),
        compiler_params=_cparams(("arbitrary", "arbitrary")),
        name="sample_attn",
    )(page_table_flat, sel_flat, q, k_own, v_own, *([cache_k] * len(pages)), *([cache_v] * len(pages)))


def _rope_tables(pos):
    half = ATT_HEAD_DIM // 2
    inv = ROPE_THETA ** (-jnp.arange(half, dtype=F32) / half)
    ang = pos.astype(F32)[:, None] * inv[None, :]
    cos = jnp.cos(ang)
    sin = jnp.sin(ang)
    reps = LANES // ATT_HEAD_DIM
    return (jnp.tile(jnp.concatenate([cos, cos], axis=1), (1, reps)),
            jnp.tile(jnp.concatenate([-sin, sin], axis=1), (1, reps)))


def _reorder_w_in(w):
    offs = [0]
    for width in (SSD_D, SSD_CONV_CH, SSD_HEADS, RET_QK, RET_QK, RET_V, RET_V):
        offs.append(offs[-1] + width)
    z, xbc, dt, rq, rk, rv, rg = [w[:, offs[i]:offs[i + 1]] for i in range(7)]
    dt = jnp.pad(dt, ((0, 0), (0, DT_PAD - SSD_HEADS)))
    return jnp.concatenate([z, rv, rg, xbc, rq, rk, dt], axis=1)


def _pad_rows(x, rows):
    return jnp.pad(x, [(0, 0)] * (x.ndim - 2) + [(0, rows - x.shape[-2]), (0, 0)])


def kernel(x_prompt, x_sample, p_prompt, p_sample, state_ssm, state_conv, state_ret, cache_k, cache_v, page_table,
           norm_mix, norm_ffn, norm_ple, ple_proj, ple_gate, hyb_w_in, hyb_w_out, ssd_conv_w, ssd_conv_b,
           ssd_dt_bias, ssd_a_log, ssd_d, ssd_norm, ret_norm, ffn_w_gate, ffn_w_up, ffn_w_down,
           att_w_in, att_w_out, att_q_norm, att_k_norm, moe_router, moe_w_gate, moe_w_up, moe_w_down):
    bp, sp, d = x_prompt.shape
    bs, ts, _ = x_sample.shape
    depth = norm_mix.shape[0]
    n_pages = page_table.shape[1]
    past = n_pages * PAGE_SIZE
    assert sp % MOBA_BLOCK == 0 and past % MOBA_BLOCK == 0 and past >= MOBA_TOPK * MOBA_BLOCK
    assert ts <= SAMPLE_ROWS and math.gcd(ts, SCAN_CHUNK) == ts
    rs = SAMPLE_ROWS

    hp = x_prompt.reshape(bp * sp, d)
    hs = _pad_rows(x_sample, rs).reshape(bs * rs, d)
    pp = p_prompt.reshape(depth, bp * sp, P_DIM)
    ps = _pad_rows(p_sample, rs).reshape(depth, bs * rs, P_DIM)
    cos_p, sin_p = _rope_tables(jnp.arange(sp))
    cos_s, sin_s = _rope_tables(past + jnp.arange(rs))
    pt_flat = page_table.reshape(-1)
    page_means = _page_mean(cache_k, 8)

    tm_p, tm_s = 512, bs * rs
    n_chunks = sp // SCAN_CHUNK
    outs = {k: [] for k in ("ssm_p", "conv_p", "ret_p", "k_p", "v_p", "ssm_s", "conv_s", "ret_s", "k_s", "v_s")}
    for i in range(depth):
        j = i // 2
        if i % 2 == 0:
            w_in = _reorder_w_in(hyb_w_in[j])
            w_out = hyb_w_out[j]
            wts = (ssd_conv_w[j], ssd_conv_b[j], ssd_dt_bias[j], ssd_a_log[j], ssd_d[j], ssd_norm[j], ret_norm[j])
            front = SUBLANES - (SSD_CONV - 1)
            u = _norm_matmul(hp, norm_mix[i], w_in.astype(BF16), tm_p, U_WIDTH // 5)
            mix, ssm, conv, ret = _mixer(
                u, cos_p, sin_p,
                jnp.zeros((bp, SUBLANES, SSD_CONV_CH), F32),
                jnp.zeros((bp, SSD_HEADS, SSD_STATE, SSD_HEAD_DIM), F32),
                jnp.zeros((bp, RET_HEADS, RET_DK, RET_DV), F32), *wts,
                n_seq=bp, n_chunks=n_chunks, q_len=SCAN_CHUNK, n_valid=SCAN_CHUNK, pos_per_chunk=True,
                mxu=BF16)
            hp = _matmul_res(mix, w_out.astype(BF16), hp, tm_p)
            outs["ssm_p"].append(ssm); outs["conv_p"].append(conv[:, front:]); outs["ret_p"].append(ret)
            u = _norm_matmul(hs, norm_mix[i], w_in, tm_s, U_WIDTH // 5)
            mix, ssm, conv, ret = _mixer(
                u, cos_s, sin_s, jnp.pad(state_conv[j], ((0, 0), (front, 0), (0, 0))), state_ssm[j], state_ret[j],
                *wts, n_seq=bs, n_chunks=1, q_len=rs, n_valid=ts, pos_per_chunk=False, mxu=F32)
            hs = _matmul_res(mix, w_out, hs, tm_s)
            outs["ssm_s"].append(ssm); outs["conv_s"].append(conv[:, front:]); outs["ret_s"].append(ret)
            wg, wu, wd = ffn_w_gate[j][None], ffn_w_up[j][None], ffn_w_down[j][None]
            tf = wg.shape[2] // 2
            hp = _ffn(hp, norm_ffn[i], wg.astype(BF16), wu.astype(BF16), wd.astype(BF16), None, tm_p, tf)
            hs = _ffn(hs, norm_ffn[i], wg, wu, wd, None, tm_s, tf)
        else:
            w_in = att_w_in[j]
            w_out = att_w_out[j]
            qkv = _norm_matmul(hp, norm_mix[i], w_in.astype(BF16), tm_p, ATT_D)
            q, k, v = _qkv_post(qkv, att_q_norm[j], att_k_norm[j], cos_p, sin_p, bp, sp, 256, True)
            hp = _matmul_res(_moba_prompt(q, k, v).reshape(bp * sp, ATT_D), w_out.astype(BF16), hp, tm_p)
            outs["k_p"].append(k); outs["v_p"].append(v)
            qkv = _norm_matmul(hs, norm_mix[i], w_in, tm_s, ATT_D)
            q, k, v = _qkv_post(qkv, att_q_norm[j], att_k_norm[j], cos_s, sin_s, bs, rs, rs, False)
            sel = _sample_select(pt_flat, q, page_means, j, n_pages)
            sel_flat = sel[:, :, :ts, :MOBA_TOPK].reshape(-1)
            o = _sample_attn(pt_flat, sel_flat, q, k, v, cache_k, cache_v, j, n_pages, ts)
            o = o.transpose(0, 2, 1, 3).reshape(bs * rs, ATT_D)
            hs = _matmul_res(o, w_out, hs, tm_s)
            outs["k_s"].append(k[:, :, :ts]); outs["v_s"].append(v[:, :, :ts])
            wg, wu, wd = moe_w_gate[j], moe_w_up[j], moe_w_down[j]
            hp = _ffn(hp, norm_ffn[i], wg.astype(BF16), wu.astype(BF16), wd.astype(BF16), moe_router[j], tm_p,
                      wg.shape[2])
            hs = _ffn(hs, norm_ffn[i], wg, wu, wd, moe_router[j], tm_s, wg.shape[2])
        hp = _ple(hp, pp, i, norm_ple[i], ple_gate[i].astype(BF16), ple_proj[i].astype(BF16), tm_p)
        hs = _ple(hs, ps, i, norm_ple[i], ple_gate[i], ple_proj[i], tm_s)

    st = lambda name: jnp.stack(outs[name])
    return (hp.reshape(bp, sp, d), hs.reshape(bs, rs, d)[:, :ts],
            st("ssm_p"), st("conv_p"), st("ret_p"), st("k_p"), st("v_p"),
            st("ssm_s"), st("conv_s"), st("ret_s"), st("k_s"), st("v_s"))
```

```python
import functools
import math

import jax
import jax.numpy as jnp
from jax import lax
from jax.experimental import pallas as pl
from jax.experimental.pallas import tpu as pltpu

F32 = jnp.float32
BF16 = jnp.bfloat16
HIGHEST = lax.Precision.HIGHEST

D_MODEL = 1024
P_DIM = 256
EPS = 1e-6
ROPE_THETA = 10000.0
PAGE_SIZE = 128
SSD_HEADS = 16
SSD_HEAD_DIM = 64
SSD_D = SSD_HEADS * SSD_HEAD_DIM
SSD_GROUPS = 2
SSD_STATE = 128
SSD_CONV = 4
SSD_CONV_CH = SSD_D + 2 * SSD_GROUPS * SSD_STATE
RET_HEADS = 8
RET_DK = 64
RET_DV = 128
RET_QK = RET_HEADS * RET_DK
RET_V = RET_HEADS * RET_DV
MIX_WIDTH = SSD_D + RET_V
SCAN_CHUNK = 128
ATT_HEADS = 16
ATT_HEAD_DIM = 64
ATT_D = ATT_HEADS * ATT_HEAD_DIM
MOBA_BLOCK = 256
MOBA_TOPK = 3
N_EXPERTS = 8
NEG = -1e30

LANES = 128
SUBLANES = 8
SAMPLE_ROWS = 8
DT_PAD = LANES
U_Z, U_RV, U_RG, U_XBC, U_RQ, U_RK, U_DT = 0, 1024, 2048, 3072, 4608, 5120, 5632
U_WIDTH = U_DT + DT_PAD
VMEM_LIMIT = 56 * 1024 * 1024


def _cparams(sem):
    return pltpu.CompilerParams(dimension_semantics=sem, vmem_limit_bytes=VMEM_LIMIT)


def _rms(x, g):
    return x * lax.rsqrt(jnp.mean(x * x, axis=-1, keepdims=True) + EPS) * g


def _silu(x):
    return x * jax.nn.sigmoid(x)


NN = (((1,), (0,)), ((), ()))
NT = (((1,), (1,)), ((), ()))
TN = (((0,), (0,)), ((), ()))


def _mm(a, b, mxu, dims=NN):
    return lax.dot_general(a.astype(mxu), b.astype(mxu), dims, preferred_element_type=F32,
                           precision=HIGHEST if mxu == F32 else None)


def _norm_matmul_body(x_ref, g_ref, w_ref, o_ref, xn_ref):
    @pl.when(pl.program_id(1) == 0)
    def _():
        xn_ref[...] = _rms(x_ref[...], g_ref[...]).astype(xn_ref.dtype)

    o_ref[...] = _mm(xn_ref[...], w_ref[...], w_ref.dtype)


def _norm_matmul(x, g, w, tm, tn):
    m, d = x.shape
    n = w.shape[1]
    return pl.pallas_call(
        _norm_matmul_body,
        grid=(m // tm, n // tn),
        in_specs=[pl.BlockSpec((tm, d), lambda i, j: (i, 0)),
                  pl.BlockSpec((1, d), lambda i, j: (0, 0)),
                  pl.BlockSpec((d, tn), lambda i, j: (0, j))],
        out_specs=pl.BlockSpec((tm, tn), lambda i, j: (i, j)),
        out_shape=jax.ShapeDtypeStruct((m, n), F32),
        scratch_shapes=[pltpu.VMEM((tm, d), w.dtype)],
        compiler_params=_cparams(("parallel", "arbitrary")),
        name="norm_matmul",
    )(x, g.reshape(1, d), w)


def _matmul_res_body(a_ref, w_ref, r_ref, o_ref):
    o_ref[...] = r_ref[...] + _mm(a_ref[...], w_ref[...], w_ref.dtype)


def _matmul_res(a, w, res, tm):
    m, k = a.shape
    n = w.shape[1]
    return pl.pallas_call(
        _matmul_res_body,
        grid=(m // tm,),
        in_specs=[pl.BlockSpec((tm, k), lambda i: (i, 0)),
                  pl.BlockSpec((k, n), lambda i: (0, 0)),
                  pl.BlockSpec((tm, n), lambda i: (i, 0))],
        out_specs=pl.BlockSpec((tm, n), lambda i: (i, 0)),
        out_shape=jax.ShapeDtypeStruct((m, n), F32),
        compiler_params=_cparams(("parallel",)),
        name="matmul_res",
    )(a, w, res)


def _rope_lanes(x, cos, sin_signed):
    lane = lax.broadcasted_iota(jnp.int32, x.shape, 1)
    rot = jnp.where(lane % ATT_HEAD_DIM < ATT_HEAD_DIM // 2,
                    pltpu.roll(x, LANES - ATT_HEAD_DIM // 2, 1),
                    pltpu.roll(x, ATT_HEAD_DIM // 2, 1))
    return x * cos + rot * sin_signed


def _mixer_body(z_ref, rv_ref, rg_ref, xbc_ref, rq_ref, rk_ref, dt_ref, cos_ref, sin_ref,
                conv0_ref, ssm0_ref, ret0_ref, convw_ref, convb_ref, dtb_ref, alog_ref,
                dskip_ref, ssdg_ref, retg_ref,
                mix_ref, ssm_ref, conv_ref, ret_ref, *, q_len, n_valid, mxu):
    c = pl.program_id(1)

    @pl.when(c == 0)
    def _():
        ssm_ref[...] = ssm0_ref[...]
        ret_ref[...] = ret0_ref[...]
        conv_ref[...] = conv0_ref[...]

    row = lax.broadcasted_iota(jnp.int32, (q_len, 1), 0)
    valid = row < n_valid
    qi = lax.broadcasted_iota(jnp.int32, (q_len, q_len), 0)
    ki = lax.broadcasted_iota(jnp.int32, (q_len, q_len), 1)
    causal = ki <= qi

    xbc = xbc_ref[...]
    ext = jnp.concatenate([conv_ref[0], xbc], axis=0)
    acc = convb_ref[...]
    for j in range(SSD_CONV):
        lo = SUBLANES - (SSD_CONV - 1) + j
        acc = acc + ext[lo:lo + q_len] * convw_ref[j:j + 1, :]
    conv_ref[0] = ext[n_valid:n_valid + SUBLANES]
    xc = _silu(acc)
    xs = xc[:, :SSD_D]
    bs = xc[:, SSD_D:SSD_D + SSD_GROUPS * SSD_STATE]
    cs = xc[:, SSD_D + SSD_GROUPS * SSD_STATE:]

    dt_in = dt_ref[...] + dtb_ref[...]
    dt = jnp.maximum(dt_in, 0.0) + jnp.log1p(jnp.exp(-jnp.abs(dt_in)))
    dt = jnp.where(valid, dt, 0.0)
    la = dt * (-jnp.exp(alog_ref[...]))
    cum = jnp.dot(causal.astype(F32), la, preferred_element_type=F32, precision=HIGHEST)
    cum_t = cum.T
    dt_t = dt.T
    last = cum[q_len - 1:q_len, :]
    e_cum = jnp.exp(cum)
    w_all = jnp.exp(last - cum) * dt
    e_last = jnp.exp(last)
    rep = SSD_HEADS // SSD_GROUPS
    ys = []
    for g in range(SSD_GROUPS):
        c_g = cs[:, g * SSD_STATE:(g + 1) * SSD_STATE].astype(mxu)
        b_g = bs[:, g * SSD_STATE:(g + 1) * SSD_STATE].astype(mxu)
        cb = _mm(c_g, b_g, mxu, NT)
        for h in range(g * rep, (g + 1) * rep):
            diff = cum[:, h:h + 1] - cum_t[h:h + 1, :]
            m = cb * jnp.exp(jnp.where(causal, diff, -jnp.inf)) * dt_t[h:h + 1, :]
            x_h = xs[:, h * SSD_HEAD_DIM:(h + 1) * SSD_HEAD_DIM]
            s_in = ssm_ref[0, h]
            y_h = _mm(m, x_h, mxu) + e_cum[:, h:h + 1] * _mm(c_g, s_in, mxu)
            ssm_ref[0, h] = e_last[:, h:h + 1] * s_in + _mm(b_g, x_h * w_all[:, h:h + 1], mxu, TN)
            ys.append(y_h)
    y = jnp.concatenate(ys, axis=1) + dskip_ref[...] * xs
    y_ssd = _rms(y * _silu(z_ref[...]), ssdg_ref[...])

    cos = cos_ref[...]
    sin = sin_ref[...]
    cnt = jnp.minimum(row + 1, n_valid).astype(F32)
    cnt_t = jnp.minimum(ki[0:1, :] + 1, n_valid).astype(F32)
    yrs = []
    for hp in range(RET_HEADS // 2):
        sl = slice(hp * LANES, (hp + 1) * LANES)
        q2 = _rope_lanes(rq_ref[:, sl], cos, sin)
        k2 = _rope_lanes(rk_ref[:, sl], cos, sin) * (RET_DK ** -0.5)
        for h in (2 * hp, 2 * hp + 1):
            log_g = math.log1p(-(2.0 ** (-5.0 - h)))
            o = (h % 2) * RET_DK
            q_h = q2[:, o:o + RET_DK]
            k_h = k2[:, o:o + RET_DK]
            v_h = rv_ref[:, h * RET_DV:(h + 1) * RET_DV]
            dec = jnp.exp(jnp.where(causal, (cnt - cnt_t) * log_g, -jnp.inf))
            s_in = ret_ref[0, h]
            y_h = _mm(_mm(q_h, k_h, mxu, NT) * dec, v_h, mxu) + jnp.exp(cnt * log_g) * _mm(q_h, s_in, mxu)
            w_k = jnp.where(valid, jnp.exp((n_valid - cnt) * log_g), 0.0)
            ret_ref[0, h] = math.exp(n_valid * log_g) * s_in + _mm(k_h, v_h * w_k, mxu, TN)
            yrs.append(y_h * lax.rsqrt(jnp.mean(y_h * y_h, axis=-1, keepdims=True) + EPS))
    yr = jnp.concatenate(yrs, axis=1) * retg_ref[...] * _silu(rg_ref[...])
    mix_ref[...] = jnp.concatenate([y_ssd, yr], axis=1).astype(mix_ref.dtype)


def _mixer(u, cos, sin, conv0, ssm0, ret0, conv_w, conv_b, dt_bias, a_log, d_skip, ssd_g, ret_g,
           *, n_seq, n_chunks, q_len, n_valid, pos_per_chunk, mxu):
    def ub(width, off):
        return pl.BlockSpec((q_len, width), lambda b, c: (b * n_chunks + c, off // width))

    def full2(a):
        return pl.BlockSpec(a.shape, lambda b, c: (0, 0))

    pos_spec = pl.BlockSpec((q_len, LANES), (lambda b, c: (c, 0)) if pos_per_chunk else (lambda b, c: (0, 0)))
    state = lambda a: pl.BlockSpec((1,) + a.shape[1:], lambda b, c: (b,) + (0,) * (a.ndim - 1))
    pad16 = lambda v: jnp.pad(v.reshape(1, -1), ((0, 0), (0, DT_PAD - SSD_HEADS)))
    params = [conv_w, conv_b.reshape(1, -1), pad16(dt_bias), pad16(a_log),
              jnp.repeat(d_skip, SSD_HEAD_DIM).reshape(1, -1), ssd_g.reshape(1, -1), ret_g.reshape(1, -1)]
    rows = n_seq * n_chunks * q_len
    out_shape = (jax.ShapeDtypeStruct((rows, MIX_WIDTH), mxu),
                 jax.ShapeDtypeStruct(ssm0.shape, F32),
                 jax.ShapeDtypeStruct(conv0.shape, F32),
                 jax.ShapeDtypeStruct(ret0.shape, F32))
    return pl.pallas_call(
        functools.partial(_mixer_body, q_len=q_len, n_valid=n_valid, mxu=mxu),
        grid=(n_seq, n_chunks),
        in_specs=[ub(SSD_D, U_Z), ub(RET_V, U_RV), ub(RET_V, U_RG), ub(SSD_CONV_CH, U_XBC),
                  ub(RET_QK, U_RQ), ub(RET_QK, U_RK), ub(DT_PAD, U_DT), pos_spec, pos_spec,
                  state(conv0), state(ssm0), state(ret0)] + [full2(p) for p in params],
        out_specs=(pl.BlockSpec((q_len, MIX_WIDTH), lambda b, c: (b * n_chunks + c, 0)),
                   state(ssm0), state(conv0), state(ret0)),
        out_shape=out_shape,
        compiler_params=_cparams(("parallel", "arbitrary")),
        name="mixer",
    )(u, u, u, u, u, u, u, cos, sin, conv0, ssm0, ret0, *params)


def _route(xn, router):
    logits = jnp.dot(xn, router, preferred_element_type=F32, precision=HIGHEST)
    lane = lax.broadcasted_iota(jnp.int32, logits.shape, 1)
    lg = jnp.where(lane < N_EXPERTS, logits, -jnp.inf)
    m1 = jnp.max(lg, axis=1, keepdims=True)
    i1 = jnp.min(jnp.where(lg == m1, lane, LANES), axis=1, keepdims=True)
    lg2 = jnp.where(lane == i1, -jnp.inf, lg)
    m2 = jnp.max(lg2, axis=1, keepdims=True)
    i2 = jnp.min(jnp.where(lg2 == m2, lane, LANES), axis=1, keepdims=True)
    e2 = jnp.exp(m2 - m1)
    den = 1.0 + e2
    return jnp.where(lane == i1, 1.0 / den, 0.0) + jnp.where(lane == i2, e2 / den, 0.0)


def _ffn_body(*refs, routed):
    if routed:
        x_ref, g_ref, r_ref, wg_ref, wu_ref, wd_ref, o_ref, xn_ref, acc_ref, comb_ref = refs
    else:
        x_ref, g_ref, wg_ref, wu_ref, wd_ref, o_ref, xn_ref, acc_ref = refs
    e = pl.program_id(1)
    f = pl.program_id(2)

    @pl.when((e == 0) & (f == 0))
    def _():
        xn = _rms(x_ref[...], g_ref[...])
        xn_ref[...] = xn.astype(xn_ref.dtype)
        acc_ref[...] = jnp.zeros_like(acc_ref)
        if routed:
            comb_ref[...] = _route(xn, r_ref[...])

    xn = xn_ref[...]
    mxu = wg_ref.dtype
    a = _silu(_mm(xn, wg_ref[0], mxu)) * _mm(xn, wu_ref[0], mxu)
    y = _mm(a, wd_ref[0], mxu)
    if routed:
        lane = lax.broadcasted_iota(jnp.int32, comb_ref.shape, 1)
        y = y * jnp.sum(jnp.where(lane == e, comb_ref[...], 0.0), axis=1, keepdims=True)
    acc_ref[...] += y

    @pl.when((e == pl.num_programs(1) - 1) & (f == pl.num_programs(2) - 1))
    def _():
        o_ref[...] = x_ref[...] + acc_ref[...]


def _ffn(x, g, wg, wu, wd, router, tm, tf):
    m, d = x.shape
    n_e, _, f_dim = wg.shape
    routed = router is not None
    in_specs = [pl.BlockSpec((tm, d), lambda i, e, f: (i, 0)),
                pl.BlockSpec((1, d), lambda i, e, f: (0, 0))]
    args = [x, g.reshape(1, d)]
    scratch = [pltpu.VMEM((tm, d), wg.dtype), pltpu.VMEM((tm, d), F32)]
    if routed:
        in_specs.append(pl.BlockSpec((d, LANES), lambda i, e, f: (0, 0)))
        args.append(jnp.pad(router, ((0, 0), (0, LANES - n_e))))
        scratch.append(pltpu.VMEM((tm, LANES), F32))
    in_specs += [pl.BlockSpec((1, d, tf), lambda i, e, f: (e, 0, f)),
                 pl.BlockSpec((1, d, tf), lambda i, e, f: (e, 0, f)),
                 pl.BlockSpec((1, tf, d), lambda i, e, f: (e, f, 0))]
    return pl.pallas_call(
        functools.partial(_ffn_body, routed=routed),
        grid=(m // tm, n_e, f_dim // tf),
        in_specs=in_specs,
        out_specs=pl.BlockSpec((tm, d), lambda i, e, f: (i, 0)),
        out_shape=jax.ShapeDtypeStruct((m, d), F32),
        scratch_shapes=scratch,
        compiler_params=_cparams(("parallel", "arbitrary", "arbitrary")),
        name="moe" if routed else "ffn",
    )(*args, wg, wu, wd)


def _ple_body(x_ref, p_ref, g_ref, wg_ref, wp_ref, o_ref):
    x = x_ref[...]
    mxu = wg_ref.dtype
    gate = jax.nn.sigmoid(_mm(_rms(x, g_ref[...]), wg_ref[...], mxu))
    o_ref[...] = x + gate * _mm(p_ref[...], wp_ref[...], mxu)


def _ple(x, p_all, layer, g, w_gate, w_proj, tm):
    m, d = x.shape
    return pl.pallas_call(
        _ple_body,
        grid=(m // tm,),
        in_specs=[pl.BlockSpec((tm, d), lambda i: (i, 0)),
                  pl.BlockSpec((None, tm, P_DIM), lambda i: (layer, i, 0)),
                  pl.BlockSpec((1, d), lambda i: (0, 0)),
                  pl.BlockSpec((d, d), lambda i: (0, 0)),
                  pl.BlockSpec((P_DIM, d), lambda i: (0, 0))],
        out_specs=pl.BlockSpec((tm, d), lambda i: (i, 0)),
        out_shape=jax.ShapeDtypeStruct((m, d), F32),
        compiler_params=_cparams(("parallel",)),
        name="ple",
    )(x, p_all, g.reshape(1, d), w_gate, w_proj)


def _qkv_post_body(qkv_ref, qn_ref, kn_ref, cos_ref, sin_ref, q_ref, k_ref, v_ref):
    half = ATT_HEAD_DIM // 2
    cos = cos_ref[:, :ATT_HEAD_DIM]
    sin = sin_ref[:, :ATT_HEAD_DIM]

    def norm_rope(x, gain):
        y = x * lax.rsqrt(jnp.mean(x * x, axis=-1, keepdims=True) + EPS) * gain
        rot = jnp.concatenate([y[:, half:], y[:, :half]], axis=1)
        return y * cos + rot * sin

    for h in range(ATT_HEADS):
        sl = slice(h * ATT_HEAD_DIM, (h + 1) * ATT_HEAD_DIM)
        q_ref[0, h] = norm_rope(qkv_ref[:, sl], qn_ref[...]) * (ATT_HEAD_DIM ** -0.5)
        k_ref[0, h] = norm_rope(qkv_ref[:, ATT_D + h * ATT_HEAD_DIM:ATT_D + (h + 1) * ATT_HEAD_DIM],
                                kn_ref[...])
        v_ref[0, h] = qkv_ref[:, 2 * ATT_D + h * ATT_HEAD_DIM:2 * ATT_D + (h + 1) * ATT_HEAD_DIM]


def _qkv_post(qkv, qn, kn, cos, sin, n_seq, seq, ts, pos_per_tile):
    nt = seq // ts
    hm = jax.ShapeDtypeStruct((n_seq, ATT_HEADS, seq, ATT_HEAD_DIM), F32)
    hm_spec = pl.BlockSpec((1, ATT_HEADS, ts, ATT_HEAD_DIM), lambda b, t: (b, 0, t, 0))
    pos_spec = pl.BlockSpec((ts, LANES), (lambda b, t: (t, 0)) if pos_per_tile else (lambda b, t: (0, 0)))
    vec = pl.BlockSpec((1, ATT_HEAD_DIM), lambda b, t: (0, 0))
    return pl.pallas_call(
        _qkv_post_body,
        grid=(n_seq, nt),
        in_specs=[pl.BlockSpec((ts, 3 * ATT_D), lambda b, t: (b * nt + t, 0)), vec, vec, pos_spec, pos_spec],
        out_specs=(hm_spec, hm_spec, hm_spec),
        out_shape=(hm, hm, hm),
        compiler_params=_cparams(("parallel", "parallel")),
        name="qkv_post",
    )(qkv, qn.reshape(1, -1), kn.reshape(1, -1), cos, sin)


HEADS_PER_STEP = LANES // ATT_HEAD_DIM


def _moba_prompt_body(q_ref, k_ref, v_ref, o_ref, kb_ref, vb_ref, km_ref, kn_ref, *, n_blocks):
    qi = pl.program_id(2)
    blk = MOBA_BLOCK
    dh = ATT_HEAD_DIM
    tail = LANES - dh
    assert n_blocks + 1 <= tail
    shift_slack = 60.0

    @pl.when(qi == 0)
    def _():
        km_ref[...] = jnp.zeros_like(km_ref)
        r_blk = lax.broadcasted_iota(jnp.int32, (n_blocks * blk, tail), 0) // blk
        c_idx = lax.broadcasted_iota(jnp.int32, (n_blocks * blk, tail), 1)
        k_tail = ((r_blk == c_idx) | (c_idx == n_blocks)).astype(BF16)
        v_tail = (c_idx == 0).astype(BF16)
        for hh in range(HEADS_PER_STEP):
            kb = k_ref[0, hh].astype(BF16)
            kb_ref[hh] = jnp.concatenate([kb, k_tail], axis=1)
            vb_ref[hh] = jnp.concatenate([v_ref[0, hh].astype(BF16), v_tail], axis=1)
            kf = kb.astype(F32)
            k_max = jnp.sqrt(jnp.max(jnp.sum(kf * kf, axis=1, keepdims=True), axis=0, keepdims=True))
            kn_ref[hh] = jnp.broadcast_to(k_max, kn_ref.shape[1:])
            for n in range(n_blocks):
                km_ref[hh, n:n + 1, :] = jnp.mean(k_ref[0, hh, n * blk:(n + 1) * blk, :], axis=0, keepdims=True)

    lane = lax.broadcasted_iota(jnp.int32, (blk, LANES), 1)
    lane_f = lane.astype(F32)
    causal = (lax.broadcasted_iota(jnp.int32, (blk, blk), 1) <= lax.broadcasted_iota(jnp.int32, (blk, blk), 0))
    own = pl.multiple_of(qi * blk, blk)
    q_aug, q_own, acc0s = [], [], []
    lowest = None
    for hh in range(HEADS_PER_STEP):
        q = q_ref[0, hh]
        qb = q.astype(BF16)
        gate = lax.dot_general(q, km_ref[hh], NT, preferred_element_type=F32, precision=HIGHEST)
        g = jnp.where(lane < qi, gate, NEG)
        sel = jnp.zeros((blk, LANES), jnp.bool_)
        for _ in range(MOBA_TOPK):
            mx = jnp.max(g, axis=1, keepdims=True)
            idx = jnp.min(jnp.where(g == mx, lane_f, float(LANES)), axis=1, keepdims=True)
            pick = lane_f == idx
            sel = sel | (pick & (lane < qi))
            g = jnp.where(pick, -jnp.inf, g)
        qf = qb.astype(F32)
        shift = 1.01 * jnp.sqrt(jnp.sum(qf * qf, axis=1, keepdims=True)) * kn_ref[hh, 0:1, 0:1]
        shift_col = jnp.where(lane == n_blocks, -shift, 0.0)
        mask_cols = jnp.where(lane < n_blocks, jnp.where(sel, 0.0, NEG), shift_col)
        q_aug.append(jnp.concatenate([qb, mask_cols[:, :tail].astype(BF16)], axis=1))
        q_own.append(jnp.concatenate([qb, shift_col[:, :tail].astype(BF16)], axis=1))

        s = jnp.where(causal, _mm(q_own[hh], kb_ref[hh, pl.ds(own, blk), :], BF16, NT), NEG)
        acc0s.append(_mm(jnp.exp(s), vb_ref[hh, pl.ds(own, blk), :], BF16))
        low = jnp.min(jnp.max(s, axis=1, keepdims=True), axis=0, keepdims=True)
        lowest = low if lowest is None else jnp.minimum(lowest, low)

    def shifted(_):
        def body(n, accs):
            off = pl.multiple_of(n * blk, blk)
            return tuple(
                accs[hh] + _mm(jnp.exp(_mm(q_aug[hh], kb_ref[hh, pl.ds(off, blk), :], BF16, NT)),
                               vb_ref[hh, pl.ds(off, blk), :], BF16)
                for hh in range(HEADS_PER_STEP))
        return lax.fori_loop(0, qi, body, tuple(acc0s))

    def running_max(_):
        init = []
        for hh in range(HEADS_PER_STEP):
            s = jnp.where(causal, _mm(q_own[hh], kb_ref[hh, pl.ds(own, blk), :], BF16, NT), NEG)
            m0 = jnp.max(s, axis=1, keepdims=True)
            init += [m0, _mm(jnp.exp(s - m0), vb_ref[hh, pl.ds(own, blk), :], BF16)]

        def body(n, carry):
            off = pl.multiple_of(n * blk, blk)
            new = []
            for hh in range(HEADS_PER_STEP):
                m_i, acc = carry[2 * hh], carry[2 * hh + 1]
                s = _mm(q_aug[hh], kb_ref[hh, pl.ds(off, blk), :], BF16, NT)
                m_new = jnp.maximum(m_i, jnp.max(s, axis=1, keepdims=True))
                acc = jnp.exp(m_i - m_new) * acc + _mm(jnp.exp(s - m_new), vb_ref[hh, pl.ds(off, blk), :], BF16)
                new += [m_new, acc]
            return tuple(new)

        fin = lax.fori_loop(0, qi, body, tuple(init))
        return tuple(fin[2 * hh + 1] for hh in range(HEADS_PER_STEP))

    accs = lax.cond(lowest[0, 0] < -shift_slack, running_max, shifted, None)
    outs = [a[:, :dh] / a[:, dh:dh + 1] for a in accs]
    o_ref[0] = jnp.concatenate(outs, axis=1).astype(BF16)


def _moba_prompt(q, k, v):
    bn, nh, s, dh = q.shape
    n_blocks = s // MOBA_BLOCK
    hps = HEADS_PER_STEP
    return pl.pallas_call(
        functools.partial(_moba_prompt_body, n_blocks=n_blocks),
        grid=(bn, nh // hps, n_blocks),
        in_specs=[pl.BlockSpec((1, hps, MOBA_BLOCK, dh), lambda b, h, i: (b, h, i, 0)),
                  pl.BlockSpec((1, hps, s, dh), lambda b, h, i: (b, h, 0, 0)),
                  pl.BlockSpec((1, hps, s, dh), lambda b, h, i: (b, h, 0, 0))],
        out_specs=pl.BlockSpec((1, MOBA_BLOCK, hps * dh), lambda b, h, i: (b, i, h)),
        out_shape=jax.ShapeDtypeStruct((bn, s, nh * dh), BF16),
        scratch_shapes=[pltpu.VMEM((hps, s, LANES), BF16), pltpu.VMEM((hps, s, LANES), BF16),
                        pltpu.VMEM((hps, LANES, dh), F32), pltpu.VMEM((hps, SUBLANES, LANES), F32)],
        compiler_params=_cparams(("parallel", "parallel", "arbitrary")),
        name="moba_prompt",
    )(q, k, v)


def _page_mean_body(k_ref, o_ref, *, pages_per_step):
    row = lax.broadcasted_iota(jnp.int32, (ATT_HEAD_DIM, LANES), 0)
    lane = lax.broadcasted_iota(jnp.int32, (ATT_HEAD_DIM, LANES), 1)
    for p in range(pages_per_step):
        for hp in range(ATT_HEADS // HEADS_PER_STEP):
            diag = jnp.zeros((ATT_HEAD_DIM, LANES), F32)
            for k in range(HEADS_PER_STEP):
                sums = jnp.sum(k_ref[0, p, hp * HEADS_PER_STEP + k], axis=1, keepdims=True)
                diag = diag + jnp.where(lane == row + k * ATT_HEAD_DIM, sums, 0.0)
            o_ref[0, p:p + 1, hp * LANES:(hp + 1) * LANES] = (
                jnp.sum(diag, axis=0, keepdims=True) * (1.0 / PAGE_SIZE))


def _page_mean(cache_kt, pages_per_step):
    nl, n_pool, nh, dh, ps = cache_kt.shape
    assert n_pool % pages_per_step == 0
    return pl.pallas_call(
        functools.partial(_page_mean_body, pages_per_step=pages_per_step),
        grid=(nl, n_pool // pages_per_step),
        in_specs=[pl.BlockSpec((1, pages_per_step, nh, dh, ps), lambda l, i: (l, i, 0, 0, 0))],
        out_specs=pl.BlockSpec((1, pages_per_step, nh * dh), lambda l, i: (l, i, 0)),
        out_shape=jax.ShapeDtypeStruct((nl, n_pool, nh * dh), F32),
        compiler_params=_cparams(("parallel", "parallel")),
        name="page_mean",
    )(cache_kt)


def _sample_select_body(pt_ref, q_ref, pm_ref, sel_ref, gath_ref, *, n_pages):
    b = pl.program_id(0)
    n_blk = n_pages // 2
    for r in range(n_pages):
        logical = 2 * r if r < n_blk else 2 * (r - n_blk) + 1
        page = pt_ref[b * n_pages + logical]
        gath_ref[r:r + 1, :] = pm_ref[0, pl.ds(page, 1), :]
    lane = lax.broadcasted_iota(jnp.int32, (SAMPLE_ROWS, LANES), 1)
    for h in range(ATT_HEADS):
        pm_h = gath_ref[:, h * ATT_HEAD_DIM:(h + 1) * ATT_HEAD_DIM]
        gp = lax.dot_general(q_ref[0, h], pm_h, (((1,), (1,)), ((), ())),
                             preferred_element_type=F32, precision=HIGHEST)
        gate = 0.5 * (gp[:, :n_blk] + gp[:, n_blk:])
        g = jnp.concatenate([gate, jnp.full((SAMPLE_ROWS, LANES - n_blk), -jnp.inf, F32)], axis=1)
        out = jnp.zeros((SAMPLE_ROWS, LANES), jnp.int32)
        for j in range(MOBA_TOPK):
            mx = jnp.max(g, axis=1, keepdims=True)
            idx = jnp.min(jnp.where(g == mx, lane, LANES), axis=1, keepdims=True)
            out = jnp.where(lane == j, idx, out)
            g = jnp.where(lane == idx, -jnp.inf, g)
        sel_ref[0, h] = out


def _sample_select(page_table_flat, q, page_means, layer, n_pages):
    bn, nh, rows, dh = q.shape
    n_pool = page_means.shape[1]
    return pl.pallas_call(
        functools.partial(_sample_select_body, n_pages=n_pages),
        grid_spec=pltpu.PrefetchScalarGridSpec(
            num_scalar_prefetch=1,
            grid=(bn,),
            in_specs=[pl.BlockSpec((1, nh, rows, dh), lambda b, pt: (b, 0, 0, 0)),
                      pl.BlockSpec((1, n_pool, nh * dh), lambda b, pt: (layer, 0, 0))],
            out_specs=pl.BlockSpec((1, nh, rows, LANES), lambda b, pt: (b, 0, 0, 0)),
            scratch_shapes=[pltpu.VMEM((n_pages, nh * dh), F32)]),
        out_shape=jax.ShapeDtypeStruct((bn, nh, rows, LANES), jnp.int32),
        compiler_params=_cparams(("arbitrary",)),
        name="sample_select",
    )(page_table_flat, q, page_means)


PAGES_PER_BLOCK = MOBA_BLOCK // PAGE_SIZE
SEL_PAGES = MOBA_TOPK * PAGES_PER_BLOCK


def _sample_attn_body(pt_ref, sel_ref, q_ref, ko_ref, vo_ref, ck_hbm, cv_hbm, o_ref, kbuf, vbuf, sem,
                      *, layer, n_pages, n_tok, n_heads):
    step = pl.program_id(0)

    def page_copies(st, slot):
        b = st // n_heads
        h = st % n_heads
        out = []
        for t in range(n_tok):
            for j in range(MOBA_TOPK):
                blk = sel_ref[((b * n_heads + h) * n_tok + t) * MOBA_TOPK + j]
                for i in range(PAGES_PER_BLOCK):
                    page = pt_ref[b * n_pages + blk * PAGES_PER_BLOCK + i]
                    r = t * SEL_PAGES + j * PAGES_PER_BLOCK + i
                    out.append(pltpu.make_async_copy(ck_hbm.at[layer, page, h], kbuf.at[slot, r], sem.at[slot, 0, r]))
                    out.append(pltpu.make_async_copy(cv_hbm.at[layer, page, h], vbuf.at[slot, r], sem.at[slot, 1, r]))
        return out

    @pl.when(step == 0)
    def _():
        for cp in page_copies(step, 0):
            cp.start()

    @pl.when(step + 1 < pl.num_programs(0))
    def _():
        for cp in page_copies(step + 1, (step + 1) % 2):
            cp.start()

    slot = step % 2
    for cp in page_copies(step, slot):
        cp.wait()

    q = q_ref[0, 0]
    q_t = q.T
    r_i = lax.broadcasted_iota(jnp.int32, (SAMPLE_ROWS, SAMPLE_ROWS), 0)
    c_i = lax.broadcasted_iota(jnp.int32, (SAMPLE_ROWS, SAMPLE_ROWS), 1)
    s_own = jnp.where(c_i <= r_i, _mm(q, ko_ref[0, 0], F32, NT), NEG)
    neg_tail = jnp.full((1, LANES - SAMPLE_ROWS), NEG, F32)
    neg_row = jnp.full((SAMPLE_ROWS - SEL_PAGES - 1, LANES), NEG, F32)
    o_cols, p_own, dens = [], [], []
    for t in range(n_tok):
        q_col = q_t[:, t:t + 1]
        rows = [jnp.sum(kbuf[slot, t * SEL_PAGES + r] * q_col, axis=0, keepdims=True) for r in range(SEL_PAGES)]
        rows.append(jnp.concatenate([s_own[t:t + 1], neg_tail], axis=1))
        s_all = jnp.concatenate(rows + [neg_row], axis=0)
        mx = jnp.max(jnp.max(s_all, axis=1, keepdims=True), axis=0, keepdims=True)
        p = jnp.exp(s_all - mx)
        dens.append(jnp.sum(jnp.sum(p, axis=1, keepdims=True), axis=0, keepdims=True))
        acc = vbuf[slot, t * SEL_PAGES] * p[0:1]
        for r in range(1, SEL_PAGES):
            acc = acc + vbuf[slot, t * SEL_PAGES + r] * p[r:r + 1]
        o_cols.append(jnp.sum(acc, axis=1, keepdims=True))
        p_own.append(p[SEL_PAGES:SEL_PAGES + 1, :SAMPLE_ROWS])
    pad = SAMPLE_ROWS - n_tok
    o_sel = jnp.concatenate(o_cols + [jnp.zeros((ATT_HEAD_DIM, pad), F32)], axis=1).T
    o_own = _mm(jnp.concatenate(p_own + [jnp.zeros((pad, SAMPLE_ROWS), F32)], axis=0), vo_ref[0, 0], F32)
    den = jnp.concatenate(dens + [jnp.ones((pad, 1), F32)], axis=0)
    o_ref[0, 0] = (o_sel + o_own) / den


def _sample_attn(page_table_flat, sel_flat, q, k_own, v_own, cache_kt, cache_vt, layer, n_pages, n_tok):
    bn, nh, rows, dh = q.shape
    assert SEL_PAGES + 1 <= SAMPLE_ROWS and rows == SAMPLE_ROWS
    n_buf = n_tok * SEL_PAGES
    own = pl.BlockSpec((1, 1, rows, dh), lambda s, pt, sel: (s // nh, s % nh, 0, 0))
    hbm = pl.BlockSpec(memory_space=pl.ANY)
    return pl.pallas_call(
        functools.partial(_sample_attn_body, layer=layer, n_pages=n_pages, n_tok=n_tok, n_heads=nh),
        grid_spec=pltpu.PrefetchScalarGridSpec(
            num_scalar_prefetch=2,
            grid=(bn * nh,),
            in_specs=[own, own, own, hbm, hbm],
            out_specs=own,
            scratch_shapes=[pltpu.VMEM((2, n_buf, dh, PAGE_SIZE), F32),
                            pltpu.VMEM((2, n_buf, dh, PAGE_SIZE), F32),
                            pltpu.SemaphoreType.DMA((2, 2, n_buf))]),
        out_shape=jax.ShapeDtypeStruct((bn, nh, rows, dh), F32),
        compiler_params=_cparams(("arbitrary",)),
        name="sample_attn",
    )(page_table_flat, sel_flat, q, k_own, v_own, cache_kt, cache_vt)


def _rope_tables(pos):
    half = ATT_HEAD_DIM // 2
    inv = ROPE_THETA ** (-jnp.arange(half, dtype=F32) / half)
    ang = pos.astype(F32)[:, None] * inv[None, :]
    cos = jnp.cos(ang)
    sin = jnp.sin(ang)
    reps = LANES // ATT_HEAD_DIM
    return (jnp.tile(jnp.concatenate([cos, cos], axis=1), (1, reps)),
            jnp.tile(jnp.concatenate([-sin, sin], axis=1), (1, reps)))


def _reorder_w_in(w):
    offs = [0]
    for width in (SSD_D, SSD_CONV_CH, SSD_HEADS, RET_QK, RET_QK, RET_V, RET_V):
        offs.append(offs[-1] + width)
    z, xbc, dt, rq, rk, rv, rg = [w[:, offs[i]:offs[i + 1]] for i in range(7)]
    dt = jnp.pad(dt, ((0, 0), (0, DT_PAD - SSD_HEADS)))
    return jnp.concatenate([z, rv, rg, xbc, rq, rk, dt], axis=1)


def _pad_rows(x, rows):
    return jnp.pad(x, [(0, 0)] * (x.ndim - 2) + [(0, rows - x.shape[-2]), (0, 0)])


def kernel(x_prompt, x_sample, p_prompt, p_sample, state_ssm, state_conv, state_ret, cache_k, cache_v, page_table,
           norm_mix, norm_ffn, norm_ple, ple_proj, ple_gate, hyb_w_in, hyb_w_out, ssd_conv_w, ssd_conv_b,
           ssd_dt_bias, ssd_a_log, ssd_d, ssd_norm, ret_norm, ffn_w_gate, ffn_w_up, ffn_w_down,
           att_w_in, att_w_out, att_q_norm, att_k_norm, moe_router, moe_w_gate, moe_w_up, moe_w_down):
    bp, sp, d = x_prompt.shape
    bs, ts, _ = x_sample.shape
    depth = norm_mix.shape[0]
    n_pages = page_table.shape[1]
    past = n_pages * PAGE_SIZE
    assert sp % MOBA_BLOCK == 0 and past % MOBA_BLOCK == 0 and past >= MOBA_TOPK * MOBA_BLOCK
    assert ts <= SAMPLE_ROWS and math.gcd(ts, SCAN_CHUNK) == ts
    rs = SAMPLE_ROWS

    hp = x_prompt.reshape(bp * sp, d)
    hs = _pad_rows(x_sample, rs).reshape(bs * rs, d)
    pp = p_prompt.reshape(depth, bp * sp, P_DIM)
    ps = _pad_rows(p_sample, rs).reshape(depth, bs * rs, P_DIM)
    cos_p, sin_p = _rope_tables(jnp.arange(sp))
    cos_s, sin_s = _rope_tables(past + jnp.arange(rs))
    pt_flat = page_table.reshape(-1)
    cache_kt = jnp.swapaxes(cache_k, 3, 4)
    cache_vt = jnp.swapaxes(cache_v, 3, 4)
    page_means = _page_mean(cache_kt, 8)

    tm_p, tm_s = 512, bs * rs
    n_chunks = sp // SCAN_CHUNK
    outs = {k: [] for k in ("ssm_p", "conv_p", "ret_p", "k_p", "v_p", "ssm_s", "conv_s", "ret_s", "k_s", "v_s")}
    for i in range(depth):
        j = i // 2
        if i % 2 == 0:
            w_in = _reorder_w_in(hyb_w_in[j])
            w_out = hyb_w_out[j]
            wts = (ssd_conv_w[j], ssd_conv_b[j], ssd_dt_bias[j], ssd_a_log[j], ssd_d[j], ssd_norm[j], ret_norm[j])
            front = SUBLANES - (SSD_CONV - 1)
            u = _norm_matmul(hp, norm_mix[i], w_in.astype(BF16), tm_p, U_WIDTH // 5)
            mix, ssm, conv, ret = _mixer(
                u, cos_p, sin_p,
                jnp.zeros((bp, SUBLANES, SSD_CONV_CH), F32),
                jnp.zeros((bp, SSD_HEADS, SSD_STATE, SSD_HEAD_DIM), F32),
                jnp.zeros((bp, RET_HEADS, RET_DK, RET_DV), F32), *wts,
                n_seq=bp, n_chunks=n_chunks, q_len=SCAN_CHUNK, n_valid=SCAN_CHUNK, pos_per_chunk=True,
                mxu=BF16)
            hp = _matmul_res(mix, w_out.astype(BF16), hp, tm_p)
            outs["ssm_p"].append(ssm); outs["conv_p"].append(conv[:, front:]); outs["ret_p"].append(ret)
            u = _norm_matmul(hs, norm_mix[i], w_in, tm_s, U_WIDTH // 5)
            mix, ssm, conv, ret = _mixer(
                u, cos_s, sin_s, jnp.pad(state_conv[j], ((0, 0), (front, 0), (0, 0))), state_ssm[j], state_ret[j],
                *wts, n_seq=bs, n_chunks=1, q_len=rs, n_valid=ts, pos_per_chunk=False, mxu=F32)
            hs = _matmul_res(mix, w_out, hs, tm_s)
            outs["ssm_s"].append(ssm); outs["conv_s"].append(conv[:, front:]); outs["ret_s"].append(ret)
            wg, wu, wd = ffn_w_gate[j][None], ffn_w_up[j][None], ffn_w_down[j][None]
            tf = wg.shape[2] // 2
            hp = _ffn(hp, norm_ffn[i], wg.astype(BF16), wu.astype(BF16), wd.astype(BF16), None, tm_p, tf)
            hs = _ffn(hs, norm_ffn[i], wg, wu, wd, None, tm_s, tf)
        else:
            w_in = att_w_in[j]
            w_out = att_w_out[j]
            qkv = _norm_matmul(hp, norm_mix[i], w_in.astype(BF16), tm_p, ATT_D)
            q, k, v = _qkv_post(qkv, att_q_norm[j], att_k_norm[j], cos_p, sin_p, bp, sp, 256, True)
            hp = _matmul_res(_moba_prompt(q, k, v).reshape(bp * sp, ATT_D), w_out.astype(BF16), hp, tm_p)
            outs["k_p"].append(k); outs["v_p"].append(v)
            qkv = _norm_matmul(hs, norm_mix[i], w_in, tm_s, ATT_D)
            q, k, v = _qkv_post(qkv, att_q_norm[j], att_k_norm[j], cos_s, sin_s, bs, rs, rs, False)
            sel = _sample_select(pt_flat, q, page_means, j, n_pages)
            sel_flat = sel[:, :, :ts, :MOBA_TOPK].reshape(-1)
            o = _sample_attn(pt_flat, sel_flat, q, k, v, cache_kt, cache_vt, j, n_pages, ts)
            o = o.transpose(0, 2, 1, 3).reshape(bs * rs, ATT_D)
            hs = _matmul_res(o, w_out, hs, tm_s)
            outs["k_s"].append(k[:, :, :ts]); outs["v_s"].append(v[:, :, :ts])
            wg, wu, wd = moe_w_gate[j], moe_w_up[j], moe_w_down[j]
            hp = _ffn(hp, norm_ffn[i], wg.astype(BF16), wu.astype(BF16), wd.astype(BF16), moe_router[j], tm_p,
                      wg.shape[2])
            hs = _ffn(hs, norm_ffn[i], wg, wu, wd, moe_router[j], tm_s, wg.shape[2])
        hp = _ple(hp, pp, i, norm_ple[i], ple_gate[i].astype(BF16), ple_proj[i].astype(BF16), tm_p)
        hs = _ple(hs, ps, i, norm_ple[i], ple_gate[i], ple_proj[i], tm_s)

    st = lambda name: jnp.stack(outs[name])
    return (hp.reshape(bp, sp, d), hs.reshape(bs, rs, d)[:, :ts],
            st("ssm_p"), st("conv_p"), st("ret_p"), st("k_p"), st("v_p"),
            st("ssm_s"), st("conv_s"), st("ret_s"), st("k_s"), st("v_s"))
```

```python
import functools
import math

import jax
import jax.numpy as jnp
from jax import lax
from jax.experimental import pallas as pl
from jax.experimental.pallas import tpu as pltpu

F32 = jnp.float32
BF16 = jnp.bfloat16
HIGHEST = lax.Precision.HIGHEST

D_MODEL = 1024
P_DIM = 256
EPS = 1e-6
ROPE_THETA = 10000.0
PAGE_SIZE = 128
SSD_HEADS = 16
SSD_HEAD_DIM = 64
SSD_D = SSD_HEADS * SSD_HEAD_DIM
SSD_GROUPS = 2
SSD_STATE = 128
SSD_CONV = 4
SSD_CONV_CH = SSD_D + 2 * SSD_GROUPS * SSD_STATE
RET_HEADS = 8
RET_DK = 64
RET_DV = 128
RET_QK = RET_HEADS * RET_DK
RET_V = RET_HEADS * RET_DV
MIX_WIDTH = SSD_D + RET_V
SCAN_CHUNK = 128
ATT_HEADS = 16
ATT_HEAD_DIM = 64
ATT_D = ATT_HEADS * ATT_HEAD_DIM
MOBA_BLOCK = 256
MOBA_TOPK = 3
N_EXPERTS = 8
NEG = -1e30

LANES = 128
SUBLANES = 8
SAMPLE_ROWS = 8
DT_PAD = LANES
U_Z, U_RV, U_RG, U_XBC, U_RQ, U_RK, U_DT = 0, 1024, 2048, 3072, 4608, 5120, 5632
U_WIDTH = U_DT + DT_PAD
VMEM_LIMIT = 56 * 1024 * 1024


def _cparams(sem):
    return pltpu.CompilerParams(dimension_semantics=sem, vmem_limit_bytes=VMEM_LIMIT)


def _rms(x, g):
    return x * lax.rsqrt(jnp.mean(x * x, axis=-1, keepdims=True) + EPS) * g


def _silu(x):
    return x * jax.nn.sigmoid(x)


NN = (((1,), (0,)), ((), ()))
NT = (((1,), (1,)), ((), ()))
TN = (((0,), (0,)), ((), ()))


def _mm(a, b, mxu, dims=NN):
    return lax.dot_general(a.astype(mxu), b.astype(mxu), dims, preferred_element_type=F32,
                           precision=HIGHEST if mxu == F32 else None)


def _norm_matmul_body(x_ref, g_ref, w_ref, o_ref, xn_ref):
    @pl.when(pl.program_id(1) == 0)
    def _():
        xn_ref[...] = _rms(x_ref[...], g_ref[...]).astype(xn_ref.dtype)

    o_ref[...] = _mm(xn_ref[...], w_ref[...], w_ref.dtype)


def _norm_matmul(x, g, w, tm, tn):
    m, d = x.shape
    n = w.shape[1]
    return pl.pallas_call(
        _norm_matmul_body,
        grid=(m // tm, n // tn),
        in_specs=[pl.BlockSpec((tm, d), lambda i, j: (i, 0)),
                  pl.BlockSpec((1, d), lambda i, j: (0, 0)),
                  pl.BlockSpec((d, tn), lambda i, j: (0, j))],
        out_specs=pl.BlockSpec((tm, tn), lambda i, j: (i, j)),
        out_shape=jax.ShapeDtypeStruct((m, n), F32),
        scratch_shapes=[pltpu.VMEM((tm, d), w.dtype)],
        compiler_params=_cparams(("parallel", "arbitrary")),
        name="norm_matmul",
    )(x, g.reshape(1, d), w)


def _matmul_res_body(a_ref, w_ref, r_ref, o_ref):
    o_ref[...] = r_ref[...] + _mm(a_ref[...], w_ref[...], w_ref.dtype)


def _matmul_res(a, w, res, tm):
    m, k = a.shape
    n = w.shape[1]
    return pl.pallas_call(
        _matmul_res_body,
        grid=(m // tm,),
        in_specs=[pl.BlockSpec((tm, k), lambda i: (i, 0)),
                  pl.BlockSpec((k, n), lambda i: (0, 0)),
                  pl.BlockSpec((tm, n), lambda i: (i, 0))],
        out_specs=pl.BlockSpec((tm, n), lambda i: (i, 0)),
        out_shape=jax.ShapeDtypeStruct((m, n), F32),
        compiler_params=_cparams(("parallel",)),
        name="matmul_res",
    )(a, w, res)


def _rope_lanes(x, cos, sin_signed):
    lane = lax.broadcasted_iota(jnp.int32, x.shape, 1)
    rot = jnp.where(lane % ATT_HEAD_DIM < ATT_HEAD_DIM // 2,
                    pltpu.roll(x, LANES - ATT_HEAD_DIM // 2, 1),
                    pltpu.roll(x, ATT_HEAD_DIM // 2, 1))
    return x * cos + rot * sin_signed


def _mixer_body(z_ref, rv_ref, rg_ref, xbc_ref, rq_ref, rk_ref, dt_ref, cos_ref, sin_ref,
                conv0_ref, ssm0_ref, ret0_ref, convw_ref, convb_ref, dtb_ref, alog_ref,
                dskip_ref, ssdg_ref, retg_ref,
                mix_ref, ssm_ref, conv_ref, ret_ref, *, q_len, n_valid, mxu):
    c = pl.program_id(1)

    @pl.when(c == 0)
    def _():
        ssm_ref[...] = ssm0_ref[...]
        ret_ref[...] = ret0_ref[...]
        conv_ref[...] = conv0_ref[...]

    row = lax.broadcasted_iota(jnp.int32, (q_len, 1), 0)
    valid = row < n_valid
    qi = lax.broadcasted_iota(jnp.int32, (q_len, q_len), 0)
    ki = lax.broadcasted_iota(jnp.int32, (q_len, q_len), 1)
    causal = ki <= qi

    xbc = xbc_ref[...]
    ext = jnp.concatenate([conv_ref[0], xbc], axis=0)
    acc = convb_ref[...]
    for j in range(SSD_CONV):
        lo = SUBLANES - (SSD_CONV - 1) + j
        acc = acc + ext[lo:lo + q_len] * convw_ref[j:j + 1, :]
    conv_ref[0] = ext[n_valid:n_valid + SUBLANES]
    xc = _silu(acc)
    xs = xc[:, :SSD_D]
    bs = xc[:, SSD_D:SSD_D + SSD_GROUPS * SSD_STATE]
    cs = xc[:, SSD_D + SSD_GROUPS * SSD_STATE:]

    dt_in = dt_ref[...] + dtb_ref[...]
    dt = jnp.maximum(dt_in, 0.0) + jnp.log1p(jnp.exp(-jnp.abs(dt_in)))
    dt = jnp.where(valid, dt, 0.0)
    la = dt * (-jnp.exp(alog_ref[...]))
    cum = jnp.dot(causal.astype(F32), la, preferred_element_type=F32, precision=HIGHEST)
    cum_t = cum.T
    dt_t = dt.T
    last = cum[q_len - 1:q_len, :]
    e_cum = jnp.exp(cum)
    w_all = jnp.exp(last - cum) * dt
    e_last = jnp.exp(last)
    rep = SSD_HEADS // SSD_GROUPS
    ys = []
    for g in range(SSD_GROUPS):
        c_g = cs[:, g * SSD_STATE:(g + 1) * SSD_STATE].astype(mxu)
        b_g = bs[:, g * SSD_STATE:(g + 1) * SSD_STATE].astype(mxu)
        cb = _mm(c_g, b_g, mxu, NT)
        for h in range(g * rep, (g + 1) * rep):
            diff = cum[:, h:h + 1] - cum_t[h:h + 1, :]
            m = cb * jnp.exp(jnp.where(causal, diff, -jnp.inf)) * dt_t[h:h + 1, :]
            x_h = xs[:, h * SSD_HEAD_DIM:(h + 1) * SSD_HEAD_DIM]
            s_in = ssm_ref[0, h]
            y_h = _mm(m, x_h, mxu) + e_cum[:, h:h + 1] * _mm(c_g, s_in, mxu)
            ssm_ref[0, h] = e_last[:, h:h + 1] * s_in + _mm(b_g, x_h * w_all[:, h:h + 1], mxu, TN)
            ys.append(y_h)
    y = jnp.concatenate(ys, axis=1) + dskip_ref[...] * xs
    y_ssd = _rms(y * _silu(z_ref[...]), ssdg_ref[...])

    cos = cos_ref[...]
    sin = sin_ref[...]
    cnt = jnp.minimum(row + 1, n_valid).astype(F32)
    cnt_t = jnp.minimum(ki[0:1, :] + 1, n_valid).astype(F32)
    yrs = []
    for hp in range(RET_HEADS // 2):
        sl = slice(hp * LANES, (hp + 1) * LANES)
        q2 = _rope_lanes(rq_ref[:, sl], cos, sin)
        k2 = _rope_lanes(rk_ref[:, sl], cos, sin) * (RET_DK ** -0.5)
        for h in (2 * hp, 2 * hp + 1):
            log_g = math.log1p(-(2.0 ** (-5.0 - h)))
            o = (h % 2) * RET_DK
            q_h = q2[:, o:o + RET_DK]
            k_h = k2[:, o:o + RET_DK]
            v_h = rv_ref[:, h * RET_DV:(h + 1) * RET_DV]
            dec = jnp.exp(jnp.where(causal, (cnt - cnt_t) * log_g, -jnp.inf))
            s_in = ret_ref[0, h]
            y_h = _mm(_mm(q_h, k_h, mxu, NT) * dec, v_h, mxu) + jnp.exp(cnt * log_g) * _mm(q_h, s_in, mxu)
            w_k = jnp.where(valid, jnp.exp((n_valid - cnt) * log_g), 0.0)
            ret_ref[0, h] = math.exp(n_valid * log_g) * s_in + _mm(k_h, v_h * w_k, mxu, TN)
            yrs.append(y_h * lax.rsqrt(jnp.mean(y_h * y_h, axis=-1, keepdims=True) + EPS))
    yr = jnp.concatenate(yrs, axis=1) * retg_ref[...] * _silu(rg_ref[...])
    mix_ref[...] = jnp.concatenate([y_ssd, yr], axis=1).astype(mix_ref.dtype)


def _mixer(u, cos, sin, conv0, ssm0, ret0, conv_w, conv_b, dt_bias, a_log, d_skip, ssd_g, ret_g,
           *, n_seq, n_chunks, q_len, n_valid, pos_per_chunk, mxu):
    def ub(width, off):
        return pl.BlockSpec((q_len, width), lambda b, c: (b * n_chunks + c, off // width))

    def full2(a):
        return pl.BlockSpec(a.shape, lambda b, c: (0, 0))

    pos_spec = pl.BlockSpec((q_len, LANES), (lambda b, c: (c, 0)) if pos_per_chunk else (lambda b, c: (0, 0)))
    state = lambda a: pl.BlockSpec((1,) + a.shape[1:], lambda b, c: (b,) + (0,) * (a.ndim - 1))
    pad16 = lambda v: jnp.pad(v.reshape(1, -1), ((0, 0), (0, DT_PAD - SSD_HEADS)))
    params = [conv_w, conv_b.reshape(1, -1), pad16(dt_bias), pad16(a_log),
              jnp.repeat(d_skip, SSD_HEAD_DIM).reshape(1, -1), ssd_g.reshape(1, -1), ret_g.reshape(1, -1)]
    rows = n_seq * n_chunks * q_len
    out_shape = (jax.ShapeDtypeStruct((rows, MIX_WIDTH), mxu),
                 jax.ShapeDtypeStruct(ssm0.shape, F32),
                 jax.ShapeDtypeStruct(conv0.shape, F32),
                 jax.ShapeDtypeStruct(ret0.shape, F32))
    return pl.pallas_call(
        functools.partial(_mixer_body, q_len=q_len, n_valid=n_valid, mxu=mxu),
        grid=(n_seq, n_chunks),
        in_specs=[ub(SSD_D, U_Z), ub(RET_V, U_RV), ub(RET_V, U_RG), ub(SSD_CONV_CH, U_XBC),
                  ub(RET_QK, U_RQ), ub(RET_QK, U_RK), ub(DT_PAD, U_DT), pos_spec, pos_spec,
                  state(conv0), state(ssm0), state(ret0)] + [full2(p) for p in params],
        out_specs=(pl.BlockSpec((q_len, MIX_WIDTH), lambda b, c: (b * n_chunks + c, 0)),
                   state(ssm0), state(conv0), state(ret0)),
        out_shape=out_shape,
        compiler_params=_cparams(("parallel", "arbitrary")),
        name="mixer",
    )(u, u, u, u, u, u, u, cos, sin, conv0, ssm0, ret0, *params)


def _route(xn, router):
    logits = jnp.dot(xn, router, preferred_element_type=F32, precision=HIGHEST)
    lane = lax.broadcasted_iota(jnp.int32, logits.shape, 1)
    lg = jnp.where(lane < N_EXPERTS, logits, -jnp.inf)
    m1 = jnp.max(lg, axis=1, keepdims=True)
    i1 = jnp.min(jnp.where(lg == m1, lane, LANES), axis=1, keepdims=True)
    lg2 = jnp.where(lane == i1, -jnp.inf, lg)
    m2 = jnp.max(lg2, axis=1, keepdims=True)
    i2 = jnp.min(jnp.where(lg2 == m2, lane, LANES), axis=1, keepdims=True)
    e2 = jnp.exp(m2 - m1)
    den = 1.0 + e2
    return jnp.where(lane == i1, 1.0 / den, 0.0) + jnp.where(lane == i2, e2 / den, 0.0)


def _ffn_body(*refs, routed):
    if routed:
        x_ref, g_ref, r_ref, wg_ref, wu_ref, wd_ref, o_ref, xn_ref, acc_ref, comb_ref = refs
    else:
        x_ref, g_ref, wg_ref, wu_ref, wd_ref, o_ref, xn_ref, acc_ref = refs
    e = pl.program_id(1)
    f = pl.program_id(2)

    @pl.when((e == 0) & (f == 0))
    def _():
        xn = _rms(x_ref[...], g_ref[...])
        xn_ref[...] = xn.astype(xn_ref.dtype)
        acc_ref[...] = jnp.zeros_like(acc_ref)
        if routed:
            comb_ref[...] = _route(xn, r_ref[...])

    xn = xn_ref[...]
    mxu = wg_ref.dtype
    a = _silu(_mm(xn, wg_ref[0], mxu)) * _mm(xn, wu_ref[0], mxu)
    y = _mm(a, wd_ref[0], mxu)
    if routed:
        lane = lax.broadcasted_iota(jnp.int32, comb_ref.shape, 1)
        y = y * jnp.sum(jnp.where(lane == e, comb_ref[...], 0.0), axis=1, keepdims=True)
    acc_ref[...] += y

    @pl.when((e == pl.num_programs(1) - 1) & (f == pl.num_programs(2) - 1))
    def _():
        o_ref[...] = x_ref[...] + acc_ref[...]


def _ffn(x, g, wg, wu, wd, router, tm, tf):
    m, d = x.shape
    n_e, _, f_dim = wg.shape
    routed = router is not None
    in_specs = [pl.BlockSpec((tm, d), lambda i, e, f: (i, 0)),
                pl.BlockSpec((1, d), lambda i, e, f: (0, 0))]
    args = [x, g.reshape(1, d)]
    scratch = [pltpu.VMEM((tm, d), wg.dtype), pltpu.VMEM((tm, d), F32)]
    if routed:
        in_specs.append(pl.BlockSpec((d, LANES), lambda i, e, f: (0, 0)))
        args.append(jnp.pad(router, ((0, 0), (0, LANES - n_e))))
        scratch.append(pltpu.VMEM((tm, LANES), F32))
    in_specs += [pl.BlockSpec((1, d, tf), lambda i, e, f: (e, 0, f)),
                 pl.BlockSpec((1, d, tf), lambda i, e, f: (e, 0, f)),
                 pl.BlockSpec((1, tf, d), lambda i, e, f: (e, f, 0))]
    return pl.pallas_call(
        functools.partial(_ffn_body, routed=routed),
        grid=(m // tm, n_e, f_dim // tf),
        in_specs=in_specs,
        out_specs=pl.BlockSpec((tm, d), lambda i, e, f: (i, 0)),
        out_shape=jax.ShapeDtypeStruct((m, d), F32),
        scratch_shapes=scratch,
        compiler_params=_cparams(("parallel", "arbitrary", "arbitrary")),
        name="moe" if routed else "ffn",
    )(*args, wg, wu, wd)


def _ple_body(x_ref, p_ref, g_ref, wg_ref, wp_ref, o_ref):
    x = x_ref[...]
    mxu = wg_ref.dtype
    gate = jax.nn.sigmoid(_mm(_rms(x, g_ref[...]), wg_ref[...], mxu))
    o_ref[...] = x + gate * _mm(p_ref[...], wp_ref[...], mxu)


def _ple(x, p_all, layer, g, w_gate, w_proj, tm):
    m, d = x.shape
    return pl.pallas_call(
        _ple_body,
        grid=(m // tm,),
        in_specs=[pl.BlockSpec((tm, d), lambda i: (i, 0)),
                  pl.BlockSpec((None, tm, P_DIM), lambda i: (layer, i, 0)),
                  pl.BlockSpec((1, d), lambda i: (0, 0)),
                  pl.BlockSpec((d, d), lambda i: (0, 0)),
                  pl.BlockSpec((P_DIM, d), lambda i: (0, 0))],
        out_specs=pl.BlockSpec((tm, d), lambda i: (i, 0)),
        out_shape=jax.ShapeDtypeStruct((m, d), F32),
        compiler_params=_cparams(("parallel",)),
        name="ple",
    )(x, p_all, g.reshape(1, d), w_gate, w_proj)


def _qkv_post_body(qkv_ref, qn_ref, kn_ref, cos_ref, sin_ref, q_ref, k_ref, v_ref):
    half = ATT_HEAD_DIM // 2
    cos = cos_ref[:, :ATT_HEAD_DIM]
    sin = sin_ref[:, :ATT_HEAD_DIM]

    def norm_rope(x, gain):
        y = x * lax.rsqrt(jnp.mean(x * x, axis=-1, keepdims=True) + EPS) * gain
        rot = jnp.concatenate([y[:, half:], y[:, :half]], axis=1)
        return y * cos + rot * sin

    for h in range(ATT_HEADS):
        sl = slice(h * ATT_HEAD_DIM, (h + 1) * ATT_HEAD_DIM)
        q_ref[0, h] = norm_rope(qkv_ref[:, sl], qn_ref[...]) * (ATT_HEAD_DIM ** -0.5)
        k_ref[0, h] = norm_rope(qkv_ref[:, ATT_D + h * ATT_HEAD_DIM:ATT_D + (h + 1) * ATT_HEAD_DIM],
                                kn_ref[...])
        v_ref[0, h] = qkv_ref[:, 2 * ATT_D + h * ATT_HEAD_DIM:2 * ATT_D + (h + 1) * ATT_HEAD_DIM]


def _qkv_post(qkv, qn, kn, cos, sin, n_seq, seq, ts, pos_per_tile):
    nt = seq // ts
    hm = jax.ShapeDtypeStruct((n_seq, ATT_HEADS, seq, ATT_HEAD_DIM), F32)
    hm_spec = pl.BlockSpec((1, ATT_HEADS, ts, ATT_HEAD_DIM), lambda b, t: (b, 0, t, 0))
    pos_spec = pl.BlockSpec((ts, LANES), (lambda b, t: (t, 0)) if pos_per_tile else (lambda b, t: (0, 0)))
    vec = pl.BlockSpec((1, ATT_HEAD_DIM), lambda b, t: (0, 0))
    return pl.pallas_call(
        _qkv_post_body,
        grid=(n_seq, nt),
        in_specs=[pl.BlockSpec((ts, 3 * ATT_D), lambda b, t: (b * nt + t, 0)), vec, vec, pos_spec, pos_spec],
        out_specs=(hm_spec, hm_spec, hm_spec),
        out_shape=(hm, hm, hm),
        compiler_params=_cparams(("parallel", "parallel")),
        name="qkv_post",
    )(qkv, qn.reshape(1, -1), kn.reshape(1, -1), cos, sin)


HEADS_PER_STEP = LANES // ATT_HEAD_DIM


def _moba_prompt_body(q_ref, k_ref, v_ref, o_ref, kb_ref, vb_ref, km_ref, kn_ref, *, n_blocks):
    qi = pl.program_id(2)
    blk = MOBA_BLOCK
    dh = ATT_HEAD_DIM
    tail = LANES - dh
    nb8 = -(-n_blocks // SUBLANES) * SUBLANES
    assert nb8 + SUBLANES <= tail
    shift_slack = 60.0

    @pl.when(qi == 0)
    def _():
        km_ref[...] = jnp.zeros_like(km_ref)
        r_blk = lax.broadcasted_iota(jnp.int32, (n_blocks * blk, tail), 0) // blk
        c_idx = lax.broadcasted_iota(jnp.int32, (n_blocks * blk, tail), 1)
        k_tail = ((r_blk == c_idx) | (c_idx == nb8)).astype(BF16)
        v_tail = (c_idx == 0).astype(BF16)
        for hh in range(HEADS_PER_STEP):
            kb = k_ref[0, hh].astype(BF16)
            kb_ref[hh] = jnp.concatenate([kb, k_tail], axis=1)
            vb_ref[hh] = jnp.concatenate([v_ref[0, hh].astype(BF16), v_tail], axis=1)
            kf = kb.astype(F32)
            k_max = jnp.sqrt(jnp.max(jnp.sum(kf * kf, axis=1, keepdims=True), axis=0, keepdims=True))
            kn_ref[hh] = jnp.broadcast_to(k_max, kn_ref.shape[1:])
            for n in range(n_blocks):
                km_ref[hh, n:n + 1, :] = jnp.mean(k_ref[0, hh, n * blk:(n + 1) * blk, :], axis=0, keepdims=True)

    causal = (lax.broadcasted_iota(jnp.int32, (blk, blk), 1) <= lax.broadcasted_iota(jnp.int32, (blk, blk), 0))
    own = pl.multiple_of(qi * blk, blk)
    blk_i = lax.broadcasted_iota(jnp.int32, (nb8, blk), 0)
    blk_f = blk_i.astype(F32)
    row8 = lax.broadcasted_iota(jnp.int32, (SUBLANES, blk), 0)
    tail_lane = lax.broadcasted_iota(jnp.int32, (blk, tail), 1)
    q_aug, q_own, acc0s = [], [], []
    lowest = None
    for hh in range(HEADS_PER_STEP):
        q = q_ref[0, hh]
        qb = q.astype(BF16)
        gate = lax.dot_general(km_ref[hh, :nb8, :], q, NT, preferred_element_type=F32, precision=HIGHEST)
        g = jnp.where(blk_i < qi, gate, NEG)
        sel = jnp.zeros((nb8, blk), jnp.bool_)
        for _ in range(MOBA_TOPK):
            mx = jnp.max(g, axis=0, keepdims=True)
            idx = jnp.min(jnp.where(g == mx, blk_f, float(LANES)), axis=0, keepdims=True)
            pick = blk_f == idx
            sel = sel | (pick & (blk_i < qi))
            g = jnp.where(pick, -jnp.inf, g)
        qf = qb.astype(F32)
        q_sq = lax.dot_general(jnp.ones((SUBLANES, dh), F32), qf * qf, NT, preferred_element_type=F32,
                               precision=HIGHEST)
        shift = 1.01 * jnp.sqrt(q_sq) * kn_ref[hh, 0:1, 0:1]
        cols_t = jnp.concatenate([jnp.where(sel, 0.0, NEG), jnp.where(row8 == 0, -shift, 0.0),
                                  jnp.zeros((tail - nb8 - SUBLANES, blk), F32)], axis=0)
        cols = cols_t.T
        q_aug.append(jnp.concatenate([qb, cols.astype(BF16)], axis=1))
        q_own.append(jnp.concatenate([qb, jnp.where(tail_lane < nb8, 0.0, cols).astype(BF16)], axis=1))

        s = jnp.where(causal, _mm(q_own[hh], kb_ref[hh, pl.ds(own, blk), :], BF16, NT), NEG)
        acc0s.append(_mm(jnp.exp(s), vb_ref[hh, pl.ds(own, blk), :], BF16))
        low = jnp.min(jnp.max(s, axis=1, keepdims=True), axis=0, keepdims=True)
        lowest = low if lowest is None else jnp.minimum(lowest, low)

    def shifted(_):
        def add_keys(off, width, accs):
            return tuple(
                accs[hh] + _mm(jnp.exp(_mm(q_aug[hh], kb_ref[hh, pl.ds(off, width), :], BF16, NT)),
                               vb_ref[hh, pl.ds(off, width), :], BF16)
                for hh in range(HEADS_PER_STEP))

        accs = lax.fori_loop(0, qi // 2, lambda i, a: add_keys(pl.multiple_of(i * 2 * blk, 2 * blk), 2 * blk, a),
                             tuple(acc0s))
        return lax.cond(qi % 2 == 1, lambda a: add_keys(pl.multiple_of((qi - 1) * blk, blk), blk, a),
                        lambda a: a, accs)

    def running_max(_):
        init = []
        for hh in range(HEADS_PER_STEP):
            s = jnp.where(causal, _mm(q_own[hh], kb_ref[hh, pl.ds(own, blk), :], BF16, NT), NEG)
            m0 = jnp.max(s, axis=1, keepdims=True)
            init += [m0, _mm(jnp.exp(s - m0), vb_ref[hh, pl.ds(own, blk), :], BF16)]

        def body(n, carry):
            off = pl.multiple_of(n * blk, blk)
            new = []
            for hh in range(HEADS_PER_STEP):
                m_i, acc = carry[2 * hh], carry[2 * hh + 1]
                s = _mm(q_aug[hh], kb_ref[hh, pl.ds(off, blk), :], BF16, NT)
                m_new = jnp.maximum(m_i, jnp.max(s, axis=1, keepdims=True))
                acc = jnp.exp(m_i - m_new) * acc + _mm(jnp.exp(s - m_new), vb_ref[hh, pl.ds(off, blk), :], BF16)
                new += [m_new, acc]
            return tuple(new)

        fin = lax.fori_loop(0, qi, body, tuple(init))
        return tuple(fin[2 * hh + 1] for hh in range(HEADS_PER_STEP))

    accs = lax.cond(lowest[0, 0] < -shift_slack, running_max, shifted, None)
    outs = [a[:, :dh] / a[:, dh:dh + 1] for a in accs]
    o_ref[0] = jnp.concatenate(outs, axis=1).astype(BF16)


def _moba_prompt(q, k, v):
    bn, nh, s, dh = q.shape
    n_blocks = s // MOBA_BLOCK
    hps = HEADS_PER_STEP
    return pl.pallas_call(
        functools.partial(_moba_prompt_body, n_blocks=n_blocks),
        grid=(bn, nh // hps, n_blocks),
        in_specs=[pl.BlockSpec((1, hps, MOBA_BLOCK, dh), lambda b, h, i: (b, h, i, 0)),
                  pl.BlockSpec((1, hps, s, dh), lambda b, h, i: (b, h, 0, 0)),
                  pl.BlockSpec((1, hps, s, dh), lambda b, h, i: (b, h, 0, 0))],
        out_specs=pl.BlockSpec((1, MOBA_BLOCK, hps * dh), lambda b, h, i: (b, i, h)),
        out_shape=jax.ShapeDtypeStruct((bn, s, nh * dh), BF16),
        scratch_shapes=[pltpu.VMEM((hps, s, LANES), BF16), pltpu.VMEM((hps, s, LANES), BF16),
                        pltpu.VMEM((hps, LANES, dh), F32), pltpu.VMEM((hps, SUBLANES, LANES), F32)],
        compiler_params=_cparams(("parallel", "parallel", "arbitrary")),
        name="moba_prompt",
    )(q, k, v)


def _page_mean_body(k_ref, o_ref, *, pages_per_step):
    row = lax.broadcasted_iota(jnp.int32, (ATT_HEAD_DIM, LANES), 0)
    lane = lax.broadcasted_iota(jnp.int32, (ATT_HEAD_DIM, LANES), 1)
    for p in range(pages_per_step):
        for hp in range(ATT_HEADS // HEADS_PER_STEP):
            diag = jnp.zeros((ATT_HEAD_DIM, LANES), F32)
            for k in range(HEADS_PER_STEP):
                sums = jnp.sum(k_ref[0, p, hp * HEADS_PER_STEP + k], axis=1, keepdims=True)
                diag = diag + jnp.where(lane == row + k * ATT_HEAD_DIM, sums, 0.0)
            o_ref[0, p:p + 1, hp * LANES:(hp + 1) * LANES] = (
                jnp.sum(diag, axis=0, keepdims=True) * (1.0 / PAGE_SIZE))


def _page_mean(cache_kt, pages_per_step):
    nl, n_pool, nh, dh, ps = cache_kt.shape
    assert n_pool % pages_per_step == 0
    return pl.pallas_call(
        functools.partial(_page_mean_body, pages_per_step=pages_per_step),
        grid=(nl, n_pool // pages_per_step),
        in_specs=[pl.BlockSpec((1, pages_per_step, nh, dh, ps), lambda l, i: (l, i, 0, 0, 0))],
        out_specs=pl.BlockSpec((1, pages_per_step, nh * dh), lambda l, i: (l, i, 0)),
        out_shape=jax.ShapeDtypeStruct((nl, n_pool, nh * dh), F32),
        compiler_params=_cparams(("parallel", "parallel")),
        name="page_mean",
    )(cache_kt)


def _sample_select_body(pt_ref, q_ref, pm_ref, sel_ref, gath_ref, *, n_pages):
    b = pl.program_id(0)
    n_blk = n_pages // 2
    for r in range(n_pages):
        logical = 2 * r if r < n_blk else 2 * (r - n_blk) + 1
        page = pt_ref[b * n_pages + logical]
        gath_ref[r:r + 1, :] = pm_ref[0, pl.ds(page, 1), :]
    lane = lax.broadcasted_iota(jnp.int32, (SAMPLE_ROWS, LANES), 1)
    for h in range(ATT_HEADS):
        pm_h = gath_ref[:, h * ATT_HEAD_DIM:(h + 1) * ATT_HEAD_DIM]
        gp = lax.dot_general(q_ref[0, h], pm_h, (((1,), (1,)), ((), ())),
                             preferred_element_type=F32, precision=HIGHEST)
        gate = 0.5 * (gp[:, :n_blk] + gp[:, n_blk:])
        g = jnp.concatenate([gate, jnp.full((SAMPLE_ROWS, LANES - n_blk), -jnp.inf, F32)], axis=1)
        out = jnp.zeros((SAMPLE_ROWS, LANES), jnp.int32)
        for j in range(MOBA_TOPK):
            mx = jnp.max(g, axis=1, keepdims=True)
            idx = jnp.min(jnp.where(g == mx, lane, LANES), axis=1, keepdims=True)
            out = jnp.where(lane == j, idx, out)
            g = jnp.where(lane == idx, -jnp.inf, g)
        sel_ref[0, h] = out


def _sample_select(page_table_flat, q, page_means, layer, n_pages):
    bn, nh, rows, dh = q.shape
    n_pool = page_means.shape[1]
    return pl.pallas_call(
        functools.partial(_sample_select_body, n_pages=n_pages),
        grid_spec=pltpu.PrefetchScalarGridSpec(
            num_scalar_prefetch=1,
            grid=(bn,),
            in_specs=[pl.BlockSpec((1, nh, rows, dh), lambda b, pt: (b, 0, 0, 0)),
                      pl.BlockSpec((1, n_pool, nh * dh), lambda b, pt: (layer, 0, 0))],
            out_specs=pl.BlockSpec((1, nh, rows, LANES), lambda b, pt: (b, 0, 0, 0)),
            scratch_shapes=[pltpu.VMEM((n_pages, nh * dh), F32)]),
        out_shape=jax.ShapeDtypeStruct((bn, nh, rows, LANES), jnp.int32),
        compiler_params=_cparams(("arbitrary",)),
        name="sample_select",
    )(page_table_flat, q, page_means)


PAGES_PER_BLOCK = MOBA_BLOCK // PAGE_SIZE
SEL_PAGES = MOBA_TOPK * PAGES_PER_BLOCK


def _sample_attn_body(pt_ref, sel_ref, q_ref, ko_ref, vo_ref, ck_hbm, cv_hbm, o_ref, kbuf, vbuf, sem,
                      *, layer, n_pages, n_tok, n_heads):
    step = pl.program_id(0)

    def page_copies(st, slot):
        b = st // n_heads
        h = st % n_heads
        out = []
        for t in range(n_tok):
            for j in range(MOBA_TOPK):
                blk = sel_ref[((b * n_heads + h) * n_tok + t) * MOBA_TOPK + j]
                for i in range(PAGES_PER_BLOCK):
                    page = pt_ref[b * n_pages + blk * PAGES_PER_BLOCK + i]
                    r = t * SEL_PAGES + j * PAGES_PER_BLOCK + i
                    out.append(pltpu.make_async_copy(ck_hbm.at[layer, page, h], kbuf.at[slot, r], sem.at[slot, 0, r]))
                    out.append(pltpu.make_async_copy(cv_hbm.at[layer, page, h], vbuf.at[slot, r], sem.at[slot, 1, r]))
        return out

    @pl.when(step == 0)
    def _():
        for cp in page_copies(step, 0):
            cp.start()

    @pl.when(step + 1 < pl.num_programs(0))
    def _():
        for cp in page_copies(step + 1, (step + 1) % 2):
            cp.start()

    slot = step % 2
    for cp in page_copies(step, slot):
        cp.wait()

    q = q_ref[0, 0]
    q_t = q.T
    r_i = lax.broadcasted_iota(jnp.int32, (SAMPLE_ROWS, SAMPLE_ROWS), 0)
    c_i = lax.broadcasted_iota(jnp.int32, (SAMPLE_ROWS, SAMPLE_ROWS), 1)
    s_own = jnp.where(c_i <= r_i, _mm(q, ko_ref[0, 0], F32, NT), NEG)
    neg_tail = jnp.full((1, LANES - SAMPLE_ROWS), NEG, F32)
    neg_row = jnp.full((SAMPLE_ROWS - SEL_PAGES - 1, LANES), NEG, F32)
    o_cols, p_own, dens = [], [], []
    for t in range(n_tok):
        q_col = q_t[:, t:t + 1]
        rows = [jnp.sum(kbuf[slot, t * SEL_PAGES + r] * q_col, axis=0, keepdims=True) for r in range(SEL_PAGES)]
        rows.append(jnp.concatenate([s_own[t:t + 1], neg_tail], axis=1))
        s_all = jnp.concatenate(rows + [neg_row], axis=0)
        mx = jnp.max(jnp.max(s_all, axis=1, keepdims=True), axis=0, keepdims=True)
        p = jnp.exp(s_all - mx)
        dens.append(jnp.sum(jnp.sum(p, axis=1, keepdims=True), axis=0, keepdims=True))
        acc = vbuf[slot, t * SEL_PAGES] * p[0:1]
        for r in range(1, SEL_PAGES):
            acc = acc + vbuf[slot, t * SEL_PAGES + r] * p[r:r + 1]
        o_cols.append(jnp.sum(acc, axis=1, keepdims=True))
        p_own.append(p[SEL_PAGES:SEL_PAGES + 1, :SAMPLE_ROWS])
    pad = SAMPLE_ROWS - n_tok
    o_sel = jnp.concatenate(o_cols + [jnp.zeros((ATT_HEAD_DIM, pad), F32)], axis=1).T
    o_own = _mm(jnp.concatenate(p_own + [jnp.zeros((pad, SAMPLE_ROWS), F32)], axis=0), vo_ref[0, 0], F32)
    den = jnp.concatenate(dens + [jnp.ones((pad, 1), F32)], axis=0)
    o_ref[0, 0] = (o_sel + o_own) / den


def _sample_attn(page_table_flat, sel_flat, q, k_own, v_own, cache_kt, cache_vt, layer, n_pages, n_tok):
    bn, nh, rows, dh = q.shape
    assert SEL_PAGES + 1 <= SAMPLE_ROWS and rows == SAMPLE_ROWS
    n_buf = n_tok * SEL_PAGES
    own = pl.BlockSpec((1, 1, rows, dh), lambda s, pt, sel: (s // nh, s % nh, 0, 0))
    hbm = pl.BlockSpec(memory_space=pl.ANY)
    return pl.pallas_call(
        functools.partial(_sample_attn_body, layer=layer, n_pages=n_pages, n_tok=n_tok, n_heads=nh),
        grid_spec=pltpu.PrefetchScalarGridSpec(
            num_scalar_prefetch=2,
            grid=(bn * nh,),
            in_specs=[own, own, own, hbm, hbm],
            out_specs=own,
            scratch_shapes=[pltpu.VMEM((2, n_buf, dh, PAGE_SIZE), F32),
                            pltpu.VMEM((2, n_buf, dh, PAGE_SIZE), F32),
                            pltpu.SemaphoreType.DMA((2, 2, n_buf))]),
        out_shape=jax.ShapeDtypeStruct((bn, nh, rows, dh), F32),
        compiler_params=_cparams(("arbitrary",)),
        name="sample_attn",
    )(page_table_flat, sel_flat, q, k_own, v_own, cache_kt, cache_vt)


def _rope_tables(pos):
    half = ATT_HEAD_DIM // 2
    inv = ROPE_THETA ** (-jnp.arange(half, dtype=F32) / half)
    ang = pos.astype(F32)[:, None] * inv[None, :]
    cos = jnp.cos(ang)
    sin = jnp.sin(ang)
    reps = LANES // ATT_HEAD_DIM
    return (jnp.tile(jnp.concatenate([cos, cos], axis=1), (1, reps)),
            jnp.tile(jnp.concatenate([-sin, sin], axis=1), (1, reps)))


def _reorder_w_in(w):
    offs = [0]
    for width in (SSD_D, SSD_CONV_CH, SSD_HEADS, RET_QK, RET_QK, RET_V, RET_V):
        offs.append(offs[-1] + width)
    z, xbc, dt, rq, rk, rv, rg = [w[:, offs[i]:offs[i + 1]] for i in range(7)]
    dt = jnp.pad(dt, ((0, 0), (0, DT_PAD - SSD_HEADS)))
    return jnp.concatenate([z, rv, rg, xbc, rq, rk, dt], axis=1)


def _pad_rows(x, rows):
    return jnp.pad(x, [(0, 0)] * (x.ndim - 2) + [(0, rows - x.shape[-2]), (0, 0)])


def kernel(x_prompt, x_sample, p_prompt, p_sample, state_ssm, state_conv, state_ret, cache_k, cache_v, page_table,
           norm_mix, norm_ffn, norm_ple, ple_proj, ple_gate, hyb_w_in, hyb_w_out, ssd_conv_w, ssd_conv_b,
           ssd_dt_bias, ssd_a_log, ssd_d, ssd_norm, ret_norm, ffn_w_gate, ffn_w_up, ffn_w_down,
           att_w_in, att_w_out, att_q_norm, att_k_norm, moe_router, moe_w_gate, moe_w_up, moe_w_down):
    bp, sp, d = x_prompt.shape
    bs, ts, _ = x_sample.shape
    depth = norm_mix.shape[0]
    n_pages = page_table.shape[1]
    past = n_pages * PAGE_SIZE
    assert sp % MOBA_BLOCK == 0 and past % MOBA_BLOCK == 0 and past >= MOBA_TOPK * MOBA_BLOCK
    assert ts <= SAMPLE_ROWS and math.gcd(ts, SCAN_CHUNK) == ts
    rs = SAMPLE_ROWS

    hp = x_prompt.reshape(bp * sp, d)
    hs = _pad_rows(x_sample, rs).reshape(bs * rs, d)
    pp = p_prompt.reshape(depth, bp * sp, P_DIM)
    ps = _pad_rows(p_sample, rs).reshape(depth, bs * rs, P_DIM)
    cos_p, sin_p = _rope_tables(jnp.arange(sp))
    cos_s, sin_s = _rope_tables(past + jnp.arange(rs))
    pt_flat = page_table.reshape(-1)
    cache_kt = jnp.swapaxes(cache_k, 3, 4)
    cache_vt = jnp.swapaxes(cache_v, 3, 4)
    page_means = _page_mean(cache_kt, 8)

    tm_p, tm_s = 512, bs * rs
    n_chunks = sp // SCAN_CHUNK
    outs = {k: [] for k in ("ssm_p", "conv_p", "ret_p", "k_p", "v_p", "ssm_s", "conv_s", "ret_s", "k_s", "v_s")}
    for i in range(depth):
        j = i // 2
        if i % 2 == 0:
            w_in = _reorder_w_in(hyb_w_in[j])
            w_out = hyb_w_out[j]
            wts = (ssd_conv_w[j], ssd_conv_b[j], ssd_dt_bias[j], ssd_a_log[j], ssd_d[j], ssd_norm[j], ret_norm[j])
            front = SUBLANES - (SSD_CONV - 1)
            u = _norm_matmul(hp, norm_mix[i], w_in.astype(BF16), tm_p, U_WIDTH // 5)
            mix, ssm, conv, ret = _mixer(
                u, cos_p, sin_p,
                jnp.zeros((bp, SUBLANES, SSD_CONV_CH), F32),
                jnp.zeros((bp, SSD_HEADS, SSD_STATE, SSD_HEAD_DIM), F32),
                jnp.zeros((bp, RET_HEADS, RET_DK, RET_DV), F32), *wts,
                n_seq=bp, n_chunks=n_chunks, q_len=SCAN_CHUNK, n_valid=SCAN_CHUNK, pos_per_chunk=True,
                mxu=BF16)
            hp = _matmul_res(mix, w_out.astype(BF16), hp, tm_p)
            outs["ssm_p"].append(ssm); outs["conv_p"].append(conv[:, front:]); outs["ret_p"].append(ret)
            u = _norm_matmul(hs, norm_mix[i], w_in, tm_s, U_WIDTH // 5)
            mix, ssm, conv, ret = _mixer(
                u, cos_s, sin_s, jnp.pad(state_conv[j], ((0, 0), (front, 0), (0, 0))), state_ssm[j], state_ret[j],
                *wts, n_seq=bs, n_chunks=1, q_len=rs, n_valid=ts, pos_per_chunk=False, mxu=F32)
            hs = _matmul_res(mix, w_out, hs, tm_s)
            outs["ssm_s"].append(ssm); outs["conv_s"].append(conv[:, front:]); outs["ret_s"].append(ret)
            wg, wu, wd = ffn_w_gate[j][None], ffn_w_up[j][None], ffn_w_down[j][None]
            tf = wg.shape[2] // 2
            hp = _ffn(hp, norm_ffn[i], wg.astype(BF16), wu.astype(BF16), wd.astype(BF16), None, tm_p, tf)
            hs = _ffn(hs, norm_ffn[i], wg, wu, wd, None, tm_s, tf)
        else:
            w_in = att_w_in[j]
            w_out = att_w_out[j]
            qkv = _norm_matmul(hp, norm_mix[i], w_in.astype(BF16), tm_p, ATT_D)
            q, k, v = _qkv_post(qkv, att_q_norm[j], att_k_norm[j], cos_p, sin_p, bp, sp, 256, True)
            hp = _matmul_res(_moba_prompt(q, k, v).reshape(bp * sp, ATT_D), w_out.astype(BF16), hp, tm_p)
            outs["k_p"].append(k); outs["v_p"].append(v)
            qkv = _norm_matmul(hs, norm_mix[i], w_in, tm_s, ATT_D)
            q, k, v = _qkv_post(qkv, att_q_norm[j], att_k_norm[j], cos_s, sin_s, bs, rs, rs, False)
            sel = _sample_select(pt_flat, q, page_means, j, n_pages)
            sel_flat = sel[:, :, :ts, :MOBA_TOPK].reshape(-1)
            o = _sample_attn(pt_flat, sel_flat, q, k, v, cache_kt, cache_vt, j, n_pages, ts)
            o = o.transpose(0, 2, 1, 3).reshape(bs * rs, ATT_D)
            hs = _matmul_res(o, w_out, hs, tm_s)
            outs["k_s"].append(k[:, :, :ts]); outs["v_s"].append(v[:, :, :ts])
            wg, wu, wd = moe_w_gate[j], moe_w_up[j], moe_w_down[j]
            hp = _ffn(hp, norm_ffn[i], wg.astype(BF16), wu.astype(BF16), wd.astype(BF16), moe_router[j], tm_p,
                      wg.shape[2])
            hs = _ffn(hs, norm_ffn[i], wg, wu, wd, moe_router[j], tm_s, wg.shape[2])
        hp = _ple(hp, pp, i, norm_ple[i], ple_gate[i].astype(BF16), ple_proj[i].astype(BF16), tm_p)
        hs = _ple(hs, ps, i, norm_ple[i], ple_gate[i], ple_proj[i], tm_s)

    st = lambda name: jnp.stack(outs[name])
    return (hp.reshape(bp, sp, d), hs.reshape(bs, rs, d)[:, :ts],
            st("ssm_p"), st("conv_p"), st("ret_p"), st("k_p"), st("v_p"),
            st("ssm_s"), st("conv_s"), st("ret_s"), st("k_s"), st("v_s"))
```

```python
import functools
import math

import jax
import jax.numpy as jnp
from jax import lax
from jax.experimental import pallas as pl
from jax.experimental.pallas import tpu as pltpu

F32 = jnp.float32
BF16 = jnp.bfloat16
HIGHEST = lax.Precision.HIGHEST

D_MODEL = 1024
P_DIM = 256
EPS = 1e-6
ROPE_THETA = 10000.0
PAGE_SIZE = 128
SSD_HEADS = 16
SSD_HEAD_DIM = 64
SSD_D = SSD_HEADS * SSD_HEAD_DIM
SSD_GROUPS = 2
SSD_STATE = 128
SSD_CONV = 4
SSD_CONV_CH = SSD_D + 2 * SSD_GROUPS * SSD_STATE
RET_HEADS = 8
RET_DK = 64
RET_DV = 128
RET_QK = RET_HEADS * RET_DK
RET_V = RET_HEADS * RET_DV
MIX_WIDTH = SSD_D + RET_V
SCAN_CHUNK = 128
ATT_HEADS = 16
ATT_HEAD_DIM = 64
ATT_D = ATT_HEADS * ATT_HEAD_DIM
MOBA_BLOCK = 256
MOBA_TOPK = 3
N_EXPERTS = 8
NEG = -1e30

LANES = 128
SUBLANES = 8
SAMPLE_ROWS = 8
DT_PAD = LANES
U_Z, U_RV, U_RG, U_XBC, U_RQ, U_RK, U_DT = 0, 1024, 2048, 3072, 4608, 5120, 5632
U_WIDTH = U_DT + DT_PAD
VMEM_LIMIT = 56 * 1024 * 1024


def _cparams(sem):
    return pltpu.CompilerParams(dimension_semantics=sem, vmem_limit_bytes=VMEM_LIMIT)


def _rms(x, g):
    return x * lax.rsqrt(jnp.mean(x * x, axis=-1, keepdims=True) + EPS) * g


def _silu(x):
    return x * jax.nn.sigmoid(x)


NN = (((1,), (0,)), ((), ()))
NT = (((1,), (1,)), ((), ()))
TN = (((0,), (0,)), ((), ()))


def _mm(a, b, mxu, dims=NN):
    return lax.dot_general(a.astype(mxu), b.astype(mxu), dims, preferred_element_type=F32,
                           precision=HIGHEST if mxu == F32 else None)


def _norm_matmul_body(x_ref, g_ref, w_ref, o_ref, xn_ref):
    @pl.when(pl.program_id(1) == 0)
    def _():
        xn_ref[...] = _rms(x_ref[...], g_ref[...]).astype(xn_ref.dtype)

    o_ref[...] = _mm(xn_ref[...], w_ref[...], w_ref.dtype)


def _norm_matmul(x, g, w, tm, tn):
    m, d = x.shape
    n = w.shape[1]
    return pl.pallas_call(
        _norm_matmul_body,
        grid=(m // tm, n // tn),
        in_specs=[pl.BlockSpec((tm, d), lambda i, j: (i, 0)),
                  pl.BlockSpec((1, d), lambda i, j: (0, 0)),
                  pl.BlockSpec((d, tn), lambda i, j: (0, j))],
        out_specs=pl.BlockSpec((tm, tn), lambda i, j: (i, j)),
        out_shape=jax.ShapeDtypeStruct((m, n), F32),
        scratch_shapes=[pltpu.VMEM((tm, d), w.dtype)],
        compiler_params=_cparams(("parallel", "arbitrary")),
        name="norm_matmul",
    )(x, g.reshape(1, d), w)


def _matmul_res_body(a_ref, w_ref, r_ref, o_ref):
    o_ref[...] = r_ref[...] + _mm(a_ref[...], w_ref[...], w_ref.dtype)


def _matmul_res(a, w, res, tm):
    m, k = a.shape
    n = w.shape[1]
    return pl.pallas_call(
        _matmul_res_body,
        grid=(m // tm,),
        in_specs=[pl.BlockSpec((tm, k), lambda i: (i, 0)),
                  pl.BlockSpec((k, n), lambda i: (0, 0)),
                  pl.BlockSpec((tm, n), lambda i: (i, 0))],
        out_specs=pl.BlockSpec((tm, n), lambda i: (i, 0)),
        out_shape=jax.ShapeDtypeStruct((m, n), F32),
        compiler_params=_cparams(("parallel",)),
        name="matmul_res",
    )(a, w, res)


def _rope_lanes(x, cos, sin_signed):
    lane = lax.broadcasted_iota(jnp.int32, x.shape, 1)
    rot = jnp.where(lane % ATT_HEAD_DIM < ATT_HEAD_DIM // 2,
                    pltpu.roll(x, LANES - ATT_HEAD_DIM // 2, 1),
                    pltpu.roll(x, ATT_HEAD_DIM // 2, 1))
    return x * cos + rot * sin_signed


def _mixer_body(z_ref, rv_ref, rg_ref, xbc_ref, rq_ref, rk_ref, dt_ref, cos_ref, sin_ref,
                conv0_ref, ssm0_ref, ret0_ref, convw_ref, convb_ref, dtb_ref, alog_ref,
                dskip_ref, ssdg_ref, retg_ref,
                mix_ref, ssm_ref, conv_ref, ret_ref, *, q_len, n_valid, mxu):
    c = pl.program_id(1)

    @pl.when(c == 0)
    def _():
        ssm_ref[...] = ssm0_ref[...]
        ret_ref[...] = ret0_ref[...]
        conv_ref[...] = conv0_ref[...]

    row = lax.broadcasted_iota(jnp.int32, (q_len, 1), 0)
    valid = row < n_valid
    qi = lax.broadcasted_iota(jnp.int32, (q_len, q_len), 0)
    ki = lax.broadcasted_iota(jnp.int32, (q_len, q_len), 1)
    causal = ki <= qi

    xbc = xbc_ref[...]
    ext = jnp.concatenate([conv_ref[0], xbc], axis=0)
    acc = convb_ref[...]
    for j in range(SSD_CONV):
        lo = SUBLANES - (SSD_CONV - 1) + j
        acc = acc + ext[lo:lo + q_len] * convw_ref[j:j + 1, :]
    conv_ref[0] = ext[n_valid:n_valid + SUBLANES]
    xc = _silu(acc)
    xs = xc[:, :SSD_D]
    bs = xc[:, SSD_D:SSD_D + SSD_GROUPS * SSD_STATE]
    cs = xc[:, SSD_D + SSD_GROUPS * SSD_STATE:]

    dt_in = dt_ref[...] + dtb_ref[...]
    dt = jnp.maximum(dt_in, 0.0) + jnp.log1p(jnp.exp(-jnp.abs(dt_in)))
    dt = jnp.where(valid, dt, 0.0)
    la = dt * (-jnp.exp(alog_ref[...]))
    cum = jnp.dot(causal.astype(F32), la, preferred_element_type=F32, precision=HIGHEST)
    cum_t = cum.T
    dt_t = dt.T
    last = cum[q_len - 1:q_len, :]
    e_cum = jnp.exp(cum)
    w_all = jnp.exp(last - cum) * dt
    e_last = jnp.exp(last)
    rep = SSD_HEADS // SSD_GROUPS
    ys = []
    for g in range(SSD_GROUPS):
        c_g = cs[:, g * SSD_STATE:(g + 1) * SSD_STATE].astype(mxu)
        b_g = bs[:, g * SSD_STATE:(g + 1) * SSD_STATE].astype(mxu)
        cb = _mm(c_g, b_g, mxu, NT)
        for h in range(g * rep, (g + 1) * rep):
            diff = cum[:, h:h + 1] - cum_t[h:h + 1, :]
            m = cb * jnp.exp(jnp.where(causal, diff, -jnp.inf)) * dt_t[h:h + 1, :]
            x_h = xs[:, h * SSD_HEAD_DIM:(h + 1) * SSD_HEAD_DIM]
            s_in = ssm_ref[0, h]
            y_h = _mm(m, x_h, mxu) + e_cum[:, h:h + 1] * _mm(c_g, s_in, mxu)
            ssm_ref[0, h] = e_last[:, h:h + 1] * s_in + _mm(b_g, x_h * w_all[:, h:h + 1], mxu, TN)
            ys.append(y_h)
    y = jnp.concatenate(ys, axis=1) + dskip_ref[...] * xs
    y_ssd = _rms(y * _silu(z_ref[...]), ssdg_ref[...])

    cos = cos_ref[...]
    sin = sin_ref[...]
    cnt = jnp.minimum(row + 1, n_valid).astype(F32)
    cnt_t = jnp.minimum(ki[0:1, :] + 1, n_valid).astype(F32)
    yrs = []
    for hp in range(RET_HEADS // 2):
        sl = slice(hp * LANES, (hp + 1) * LANES)
        q2 = _rope_lanes(rq_ref[:, sl], cos, sin)
        k2 = _rope_lanes(rk_ref[:, sl], cos, sin) * (RET_DK ** -0.5)
        for h in (2 * hp, 2 * hp + 1):
            log_g = math.log1p(-(2.0 ** (-5.0 - h)))
            o = (h % 2) * RET_DK
            q_h = q2[:, o:o + RET_DK]
            k_h = k2[:, o:o + RET_DK]
            v_h = rv_ref[:, h * RET_DV:(h + 1) * RET_DV]
            dec = jnp.exp(jnp.where(causal, (cnt - cnt_t) * log_g, -jnp.inf))
            s_in = ret_ref[0, h]
            y_h = _mm(_mm(q_h, k_h, mxu, NT) * dec, v_h, mxu) + jnp.exp(cnt * log_g) * _mm(q_h, s_in, mxu)
            w_k = jnp.where(valid, jnp.exp((n_valid - cnt) * log_g), 0.0)
            ret_ref[0, h] = math.exp(n_valid * log_g) * s_in + _mm(k_h, v_h * w_k, mxu, TN)
            yrs.append(y_h * lax.rsqrt(jnp.mean(y_h * y_h, axis=-1, keepdims=True) + EPS))
    yr = jnp.concatenate(yrs, axis=1) * retg_ref[...] * _silu(rg_ref[...])
    mix_ref[...] = jnp.concatenate([y_ssd, yr], axis=1).astype(mix_ref.dtype)


def _mixer(u, cos, sin, conv0, ssm0, ret0, conv_w, conv_b, dt_bias, a_log, d_skip, ssd_g, ret_g,
           *, n_seq, n_chunks, q_len, n_valid, pos_per_chunk, mxu):
    def ub(width, off):
        return pl.BlockSpec((q_len, width), lambda b, c: (b * n_chunks + c, off // width))

    def full2(a):
        return pl.BlockSpec(a.shape, lambda b, c: (0, 0))

    pos_spec = pl.BlockSpec((q_len, LANES), (lambda b, c: (c, 0)) if pos_per_chunk else (lambda b, c: (0, 0)))
    state = lambda a: pl.BlockSpec((1,) + a.shape[1:], lambda b, c: (b,) + (0,) * (a.ndim - 1))
    pad16 = lambda v: jnp.pad(v.reshape(1, -1), ((0, 0), (0, DT_PAD - SSD_HEADS)))
    params = [conv_w, conv_b.reshape(1, -1), pad16(dt_bias), pad16(a_log),
              jnp.repeat(d_skip, SSD_HEAD_DIM).reshape(1, -1), ssd_g.reshape(1, -1), ret_g.reshape(1, -1)]
    rows = n_seq * n_chunks * q_len
    out_shape = (jax.ShapeDtypeStruct((rows, MIX_WIDTH), mxu),
                 jax.ShapeDtypeStruct(ssm0.shape, F32),
                 jax.ShapeDtypeStruct(conv0.shape, F32),
                 jax.ShapeDtypeStruct(ret0.shape, F32))
    return pl.pallas_call(
        functools.partial(_mixer_body, q_len=q_len, n_valid=n_valid, mxu=mxu),
        grid=(n_seq, n_chunks),
        in_specs=[ub(SSD_D, U_Z), ub(RET_V, U_RV), ub(RET_V, U_RG), ub(SSD_CONV_CH, U_XBC),
                  ub(RET_QK, U_RQ), ub(RET_QK, U_RK), ub(DT_PAD, U_DT), pos_spec, pos_spec,
                  state(conv0), state(ssm0), state(ret0)] + [full2(p) for p in params],
        out_specs=(pl.BlockSpec((q_len, MIX_WIDTH), lambda b, c: (b * n_chunks + c, 0)),
                   state(ssm0), state(conv0), state(ret0)),
        out_shape=out_shape,
        compiler_params=_cparams(("parallel", "arbitrary")),
        name="mixer",
    )(u, u, u, u, u, u, u, cos, sin, conv0, ssm0, ret0, *params)


def _route(xn, router):
    logits = jnp.dot(xn, router, preferred_element_type=F32, precision=HIGHEST)
    lane = lax.broadcasted_iota(jnp.int32, logits.shape, 1)
    lg = jnp.where(lane < N_EXPERTS, logits, -jnp.inf)
    m1 = jnp.max(lg, axis=1, keepdims=True)
    i1 = jnp.min(jnp.where(lg == m1, lane, LANES), axis=1, keepdims=True)
    lg2 = jnp.where(lane == i1, -jnp.inf, lg)
    m2 = jnp.max(lg2, axis=1, keepdims=True)
    i2 = jnp.min(jnp.where(lg2 == m2, lane, LANES), axis=1, keepdims=True)
    e2 = jnp.exp(m2 - m1)
    den = 1.0 + e2
    return jnp.where(lane == i1, 1.0 / den, 0.0) + jnp.where(lane == i2, e2 / den, 0.0)


def _ffn_body(*refs, routed):
    if routed:
        x_ref, g_ref, r_ref, wg_ref, wu_ref, wd_ref, o_ref, xn_ref, acc_ref, comb_ref = refs
    else:
        x_ref, g_ref, wg_ref, wu_ref, wd_ref, o_ref, xn_ref, acc_ref = refs
    e = pl.program_id(1)
    f = pl.program_id(2)

    @pl.when((e == 0) & (f == 0))
    def _():
        xn = _rms(x_ref[...], g_ref[...])
        xn_ref[...] = xn.astype(xn_ref.dtype)
        acc_ref[...] = jnp.zeros_like(acc_ref)
        if routed:
            comb_ref[...] = _route(xn, r_ref[...])

    xn = xn_ref[...]
    mxu = wg_ref.dtype
    a = _silu(_mm(xn, wg_ref[0], mxu)) * _mm(xn, wu_ref[0], mxu)
    y = _mm(a, wd_ref[0], mxu)
    if routed:
        lane = lax.broadcasted_iota(jnp.int32, comb_ref.shape, 1)
        y = y * jnp.sum(jnp.where(lane == e, comb_ref[...], 0.0), axis=1, keepdims=True)
    acc_ref[...] += y

    @pl.when((e == pl.num_programs(1) - 1) & (f == pl.num_programs(2) - 1))
    def _():
        o_ref[...] = x_ref[...] + acc_ref[...]


def _ffn(x, g, wg, wu, wd, router, tm, tf):
    m, d = x.shape
    n_e, _, f_dim = wg.shape
    routed = router is not None
    in_specs = [pl.BlockSpec((tm, d), lambda i, e, f: (i, 0)),
                pl.BlockSpec((1, d), lambda i, e, f: (0, 0))]
    args = [x, g.reshape(1, d)]
    scratch = [pltpu.VMEM((tm, d), wg.dtype), pltpu.VMEM((tm, d), F32)]
    if routed:
        in_specs.append(pl.BlockSpec((d, LANES), lambda i, e, f: (0, 0)))
        args.append(jnp.pad(router, ((0, 0), (0, LANES - n_e))))
        scratch.append(pltpu.VMEM((tm, LANES), F32))
    in_specs += [pl.BlockSpec((1, d, tf), lambda i, e, f: (e, 0, f)),
                 pl.BlockSpec((1, d, tf), lambda i, e, f: (e, 0, f)),
                 pl.BlockSpec((1, tf, d), lambda i, e, f: (e, f, 0))]
    return pl.pallas_call(
        functools.partial(_ffn_body, routed=routed),
        grid=(m // tm, n_e, f_dim // tf),
        in_specs=in_specs,
        out_specs=pl.BlockSpec((tm, d), lambda i, e, f: (i, 0)),
        out_shape=jax.ShapeDtypeStruct((m, d), F32),
        scratch_shapes=scratch,
        compiler_params=_cparams(("parallel", "arbitrary", "arbitrary")),
        name="moe" if routed else "ffn",
    )(*args, wg, wu, wd)


def _ple_body(x_ref, p_ref, g_ref, wg_ref, wp_ref, o_ref):
    x = x_ref[...]
    mxu = wg_ref.dtype
    gate = jax.nn.sigmoid(_mm(_rms(x, g_ref[...]), wg_ref[...], mxu))
    o_ref[...] = x + gate * _mm(p_ref[...], wp_ref[...], mxu)


def _ple(x, p_all, layer, g, w_gate, w_proj, tm):
    m, d = x.shape
    return pl.pallas_call(
        _ple_body,
        grid=(m // tm,),
        in_specs=[pl.BlockSpec((tm, d), lambda i: (i, 0)),
                  pl.BlockSpec((None, tm, P_DIM), lambda i: (layer, i, 0)),
                  pl.BlockSpec((1, d), lambda i: (0, 0)),
                  pl.BlockSpec((d, d), lambda i: (0, 0)),
                  pl.BlockSpec((P_DIM, d), lambda i: (0, 0))],
        out_specs=pl.BlockSpec((tm, d), lambda i: (i, 0)),
        out_shape=jax.ShapeDtypeStruct((m, d), F32),
        compiler_params=_cparams(("parallel",)),
        name="ple",
    )(x, p_all, g.reshape(1, d), w_gate, w_proj)


def _qkv_post_body(qkv_ref, qn_ref, kn_ref, cos_ref, sin_ref, q_ref, k_ref, v_ref):
    half = ATT_HEAD_DIM // 2
    cos = cos_ref[:, :ATT_HEAD_DIM]
    sin = sin_ref[:, :ATT_HEAD_DIM]

    def norm_rope(x, gain):
        y = x * lax.rsqrt(jnp.mean(x * x, axis=-1, keepdims=True) + EPS) * gain
        rot = jnp.concatenate([y[:, half:], y[:, :half]], axis=1)
        return y * cos + rot * sin

    for h in range(ATT_HEADS):
        sl = slice(h * ATT_HEAD_DIM, (h + 1) * ATT_HEAD_DIM)
        q_ref[0, h] = norm_rope(qkv_ref[:, sl], qn_ref[...]) * (ATT_HEAD_DIM ** -0.5)
        k_ref[0, h] = norm_rope(qkv_ref[:, ATT_D + h * ATT_HEAD_DIM:ATT_D + (h + 1) * ATT_HEAD_DIM],
                                kn_ref[...])
        v_ref[0, h] = qkv_ref[:, 2 * ATT_D + h * ATT_HEAD_DIM:2 * ATT_D + (h + 1) * ATT_HEAD_DIM]


def _qkv_post(qkv, qn, kn, cos, sin, n_seq, seq, ts, pos_per_tile):
    nt = seq // ts
    hm = jax.ShapeDtypeStruct((n_seq, ATT_HEADS, seq, ATT_HEAD_DIM), F32)
    hm_spec = pl.BlockSpec((1, ATT_HEADS, ts, ATT_HEAD_DIM), lambda b, t: (b, 0, t, 0))
    pos_spec = pl.BlockSpec((ts, LANES), (lambda b, t: (t, 0)) if pos_per_tile else (lambda b, t: (0, 0)))
    vec = pl.BlockSpec((1, ATT_HEAD_DIM), lambda b, t: (0, 0))
    return pl.pallas_call(
        _qkv_post_body,
        grid=(n_seq, nt),
        in_specs=[pl.BlockSpec((ts, 3 * ATT_D), lambda b, t: (b * nt + t, 0)), vec, vec, pos_spec, pos_spec],
        out_specs=(hm_spec, hm_spec, hm_spec),
        out_shape=(hm, hm, hm),
        compiler_params=_cparams(("parallel", "parallel")),
        name="qkv_post",
    )(qkv, qn.reshape(1, -1), kn.reshape(1, -1), cos, sin)


HEADS_PER_STEP = LANES // ATT_HEAD_DIM


def _moba_prompt_body(q_ref, k_ref, v_ref, o_ref, kb_ref, vb_ref, km_ref, qa_ref, *, n_blocks):
    qi = pl.program_id(2)
    blk = MOBA_BLOCK
    dh = ATT_HEAD_DIM
    tail = LANES - dh
    nb8 = -(-n_blocks // SUBLANES) * SUBLANES
    assert nb8 + SUBLANES <= tail
    shift_slack = 60.0

    @pl.when(qi == 0)
    def _():
        s_len = n_blocks * blk
        km_ref[...] = jnp.zeros_like(km_ref)
        r_blk = lax.broadcasted_iota(jnp.int32, (s_len, tail), 0) // blk
        c_idx = lax.broadcasted_iota(jnp.int32, (s_len, tail), 1)
        k_tail = ((r_blk == c_idx) | (c_idx == nb8)).astype(BF16)
        v_tail = (c_idx == 0).astype(BF16)
        blk_i = lax.broadcasted_iota(jnp.int32, (nb8, s_len), 0)
        blk_f = blk_i.astype(F32)
        past = blk_i < lax.broadcasted_iota(jnp.int32, (nb8, s_len), 1) // blk
        row8 = lax.broadcasted_iota(jnp.int32, (SUBLANES, s_len), 0)
        for hh in range(HEADS_PER_STEP):
            kb = k_ref[0, hh].astype(BF16)
            kb_ref[hh] = jnp.concatenate([kb, k_tail], axis=1)
            vb_ref[hh] = jnp.concatenate([v_ref[0, hh].astype(BF16), v_tail], axis=1)
            kf = kb.astype(F32)
            k_max = jnp.sqrt(jnp.max(jnp.sum(kf * kf, axis=1, keepdims=True), axis=0, keepdims=True))
            for n in range(n_blocks):
                km_ref[hh, n:n + 1, :] = jnp.mean(k_ref[0, hh, n * blk:(n + 1) * blk, :], axis=0, keepdims=True)

            q = q_ref[0, hh]
            qb = q.astype(BF16)
            gate = lax.dot_general(km_ref[hh, :nb8, :], q, NT, preferred_element_type=F32, precision=HIGHEST)
            g = jnp.where(past, gate, NEG)
            sel = jnp.zeros((nb8, s_len), jnp.bool_)
            for _ in range(MOBA_TOPK):
                mx = jnp.max(g, axis=0, keepdims=True)
                idx = jnp.min(jnp.where(g == mx, blk_f, float(LANES)), axis=0, keepdims=True)
                pick = blk_f == idx
                sel = sel | (pick & past)
                g = jnp.where(pick, -jnp.inf, g)
            qf = qb.astype(F32)
            q_sq = lax.dot_general(jnp.ones((SUBLANES, dh), F32), qf * qf, NT, preferred_element_type=F32,
                                   precision=HIGHEST)
            shift = 1.01 * jnp.sqrt(q_sq) * k_max
            cols_t = jnp.concatenate([jnp.where(sel, 0.0, NEG), jnp.where(row8 == 0, -shift, 0.0),
                                      jnp.zeros((tail - nb8 - SUBLANES, s_len), F32)], axis=0)
            for n in range(n_blocks):
                rows = slice(n * blk, (n + 1) * blk)
                qa_ref[hh, rows, :] = jnp.concatenate([qb[rows], cols_t[:, rows].T.astype(BF16)], axis=1)

    causal = (lax.broadcasted_iota(jnp.int32, (blk, blk), 1) <= lax.broadcasted_iota(jnp.int32, (blk, blk), 0))
    own = pl.multiple_of(qi * blk, blk)
    lane = lax.broadcasted_iota(jnp.int32, (blk, LANES), 1)
    mask_lanes = (lane >= dh) & (lane < dh + nb8)
    q_aug, q_own, acc0s = [], [], []
    lowest = None
    for hh in range(HEADS_PER_STEP):
        q_aug.append(qa_ref[hh, pl.ds(own, blk), :])
        q_own.append(jnp.where(mask_lanes, jnp.zeros_like(q_aug[hh]), q_aug[hh]))

        s = jnp.where(causal, _mm(q_own[hh], kb_ref[hh, pl.ds(own, blk), :], BF16, NT), NEG)
        acc0s.append(_mm(jnp.exp(s), vb_ref[hh, pl.ds(own, blk), :], BF16))
        low = jnp.min(jnp.max(s, axis=1, keepdims=True), axis=0, keepdims=True)
        lowest = low if lowest is None else jnp.minimum(lowest, low)

    def shifted(_):
        def add_keys(off, width, accs):
            return tuple(
                accs[hh] + _mm(jnp.exp(_mm(q_aug[hh], kb_ref[hh, pl.ds(off, width), :], BF16, NT)),
                               vb_ref[hh, pl.ds(off, width), :], BF16)
                for hh in range(HEADS_PER_STEP))

        accs = lax.fori_loop(0, qi // 2, lambda i, a: add_keys(pl.multiple_of(i * 2 * blk, 2 * blk), 2 * blk, a),
                             tuple(acc0s))
        return lax.cond(qi % 2 == 1, lambda a: add_keys(pl.multiple_of((qi - 1) * blk, blk), blk, a),
                        lambda a: a, accs)

    def running_max(_):
        init = []
        for hh in range(HEADS_PER_STEP):
            s = jnp.where(causal, _mm(q_own[hh], kb_ref[hh, pl.ds(own, blk), :], BF16, NT), NEG)
            m0 = jnp.max(s, axis=1, keepdims=True)
            init += [m0, _mm(jnp.exp(s - m0), vb_ref[hh, pl.ds(own, blk), :], BF16)]

        def body(n, carry):
            off = pl.multiple_of(n * blk, blk)
            new = []
            for hh in range(HEADS_PER_STEP):
                m_i, acc = carry[2 * hh], carry[2 * hh + 1]
                s = _mm(q_aug[hh], kb_ref[hh, pl.ds(off, blk), :], BF16, NT)
                m_new = jnp.maximum(m_i, jnp.max(s, axis=1, keepdims=True))
                acc = jnp.exp(m_i - m_new) * acc + _mm(jnp.exp(s - m_new), vb_ref[hh, pl.ds(off, blk), :], BF16)
                new += [m_new, acc]
            return tuple(new)

        fin = lax.fori_loop(0, qi, body, tuple(init))
        return tuple(fin[2 * hh + 1] for hh in range(HEADS_PER_STEP))

    accs = lax.cond(lowest[0, 0] < -shift_slack, running_max, shifted, None)
    outs = [a[:, :dh] / a[:, dh:dh + 1] for a in accs]
    o_ref[0] = jnp.concatenate(outs, axis=1).astype(BF16)


def _moba_prompt(q, k, v):
    bn, nh, s, dh = q.shape
    n_blocks = s // MOBA_BLOCK
    hps = HEADS_PER_STEP
    return pl.pallas_call(
        functools.partial(_moba_prompt_body, n_blocks=n_blocks),
        grid=(bn, nh // hps, n_blocks),
        in_specs=[pl.BlockSpec((1, hps, s, dh), lambda b, h, i: (b, h, 0, 0)),
                  pl.BlockSpec((1, hps, s, dh), lambda b, h, i: (b, h, 0, 0)),
                  pl.BlockSpec((1, hps, s, dh), lambda b, h, i: (b, h, 0, 0))],
        out_specs=pl.BlockSpec((1, MOBA_BLOCK, hps * dh), lambda b, h, i: (b, i, h)),
        out_shape=jax.ShapeDtypeStruct((bn, s, nh * dh), BF16),
        scratch_shapes=[pltpu.VMEM((hps, s, LANES), BF16), pltpu.VMEM((hps, s, LANES), BF16),
                        pltpu.VMEM((hps, LANES, dh), F32), pltpu.VMEM((hps, s, LANES), BF16)],
        compiler_params=_cparams(("parallel", "parallel", "arbitrary")),
        name="moba_prompt",
    )(q, k, v)


def _page_mean_body(k_ref, o_ref, *, pages_per_step):
    row = lax.broadcasted_iota(jnp.int32, (ATT_HEAD_DIM, LANES), 0)
    lane = lax.broadcasted_iota(jnp.int32, (ATT_HEAD_DIM, LANES), 1)
    for p in range(pages_per_step):
        for hp in range(ATT_HEADS // HEADS_PER_STEP):
            diag = jnp.zeros((ATT_HEAD_DIM, LANES), F32)
            for k in range(HEADS_PER_STEP):
                sums = jnp.sum(k_ref[0, p, hp * HEADS_PER_STEP + k], axis=1, keepdims=True)
                diag = diag + jnp.where(lane == row + k * ATT_HEAD_DIM, sums, 0.0)
            o_ref[0, p:p + 1, hp * LANES:(hp + 1) * LANES] = (
                jnp.sum(diag, axis=0, keepdims=True) * (1.0 / PAGE_SIZE))


def _page_mean(cache_kt, pages_per_step):
    nl, n_pool, nh, dh, ps = cache_kt.shape
    assert n_pool % pages_per_step == 0
    return pl.pallas_call(
        functools.partial(_page_mean_body, pages_per_step=pages_per_step),
        grid=(nl, n_pool // pages_per_step),
        in_specs=[pl.BlockSpec((1, pages_per_step, nh, dh, ps), lambda l, i: (l, i, 0, 0, 0))],
        out_specs=pl.BlockSpec((1, pages_per_step, nh * dh), lambda l, i: (l, i, 0)),
        out_shape=jax.ShapeDtypeStruct((nl, n_pool, nh * dh), F32),
        compiler_params=_cparams(("parallel", "parallel")),
        name="page_mean",
    )(cache_kt)


def _sample_select_body(pt_ref, q_ref, pm_ref, sel_ref, gath_ref, *, n_pages):
    b = pl.program_id(0)
    n_blk = n_pages // 2
    for r in range(n_pages):
        logical = 2 * r if r < n_blk else 2 * (r - n_blk) + 1
        page = pt_ref[b * n_pages + logical]
        gath_ref[r:r + 1, :] = pm_ref[0, pl.ds(page, 1), :]
    lane = lax.broadcasted_iota(jnp.int32, (SAMPLE_ROWS, LANES), 1)
    for h in range(ATT_HEADS):
        pm_h = gath_ref[:, h * ATT_HEAD_DIM:(h + 1) * ATT_HEAD_DIM]
        gp = lax.dot_general(q_ref[0, h], pm_h, (((1,), (1,)), ((), ())),
                             preferred_element_type=F32, precision=HIGHEST)
        gate = 0.5 * (gp[:, :n_blk] + gp[:, n_blk:])
        g = jnp.concatenate([gate, jnp.full((SAMPLE_ROWS, LANES - n_blk), -jnp.inf, F32)], axis=1)
        out = jnp.zeros((SAMPLE_ROWS, LANES), jnp.int32)
        for j in range(MOBA_TOPK):
            mx = jnp.max(g, axis=1, keepdims=True)
            idx = jnp.min(jnp.where(g == mx, lane, LANES), axis=1, keepdims=True)
            out = jnp.where(lane == j, idx, out)
            g = jnp.where(lane == idx, -jnp.inf, g)
        sel_ref[0, h] = out


def _sample_select(page_table_flat, q, page_means, layer, n_pages):
    bn, nh, rows, dh = q.shape
    n_pool = page_means.shape[1]
    return pl.pallas_call(
        functools.partial(_sample_select_body, n_pages=n_pages),
        grid_spec=pltpu.PrefetchScalarGridSpec(
            num_scalar_prefetch=1,
            grid=(bn,),
            in_specs=[pl.BlockSpec((1, nh, rows, dh), lambda b, pt: (b, 0, 0, 0)),
                      pl.BlockSpec((1, n_pool, nh * dh), lambda b, pt: (layer, 0, 0))],
            out_specs=pl.BlockSpec((1, nh, rows, LANES), lambda b, pt: (b, 0, 0, 0)),
            scratch_shapes=[pltpu.VMEM((n_pages, nh * dh), F32)]),
        out_shape=jax.ShapeDtypeStruct((bn, nh, rows, LANES), jnp.int32),
        compiler_params=_cparams(("arbitrary",)),
        name="sample_select",
    )(page_table_flat, q, page_means)


PAGES_PER_BLOCK = MOBA_BLOCK // PAGE_SIZE
SEL_PAGES = MOBA_TOPK * PAGES_PER_BLOCK


def _sample_attn_body(pt_ref, sel_ref, q_ref, ko_ref, vo_ref, ck_hbm, cv_hbm, o_ref, kbuf, vbuf, sem,
                      *, layer, n_pages, n_tok, n_heads):
    step = pl.program_id(0)

    def page_copies(st, slot):
        b = st // n_heads
        h = st % n_heads
        out = []
        for t in range(n_tok):
            for j in range(MOBA_TOPK):
                blk = sel_ref[((b * n_heads + h) * n_tok + t) * MOBA_TOPK + j]
                for i in range(PAGES_PER_BLOCK):
                    page = pt_ref[b * n_pages + blk * PAGES_PER_BLOCK + i]
                    r = t * SEL_PAGES + j * PAGES_PER_BLOCK + i
                    out.append(pltpu.make_async_copy(ck_hbm.at[layer, page, h], kbuf.at[slot, r], sem.at[slot, 0, r]))
                    out.append(pltpu.make_async_copy(cv_hbm.at[layer, page, h], vbuf.at[slot, r], sem.at[slot, 1, r]))
        return out

    @pl.when(step == 0)
    def _():
        for cp in page_copies(step, 0):
            cp.start()

    @pl.when(step + 1 < pl.num_programs(0))
    def _():
        for cp in page_copies(step + 1, (step + 1) % 2):
            cp.start()

    slot = step % 2
    for cp in page_copies(step, slot):
        cp.wait()

    q = q_ref[0, 0]
    q_t = q.T
    r_i = lax.broadcasted_iota(jnp.int32, (SAMPLE_ROWS, SAMPLE_ROWS), 0)
    c_i = lax.broadcasted_iota(jnp.int32, (SAMPLE_ROWS, SAMPLE_ROWS), 1)
    s_own = jnp.where(c_i <= r_i, _mm(q, ko_ref[0, 0], F32, NT), NEG)
    neg_tail = jnp.full((1, LANES - SAMPLE_ROWS), NEG, F32)
    neg_row = jnp.full((SAMPLE_ROWS - SEL_PAGES - 1, LANES), NEG, F32)
    o_cols, p_own, dens = [], [], []
    for t in range(n_tok):
        q_col = q_t[:, t:t + 1]
        rows = [jnp.sum(kbuf[slot, t * SEL_PAGES + r] * q_col, axis=0, keepdims=True) for r in range(SEL_PAGES)]
        rows.append(jnp.concatenate([s_own[t:t + 1], neg_tail], axis=1))
        s_all = jnp.concatenate(rows + [neg_row], axis=0)
        mx = jnp.max(jnp.max(s_all, axis=1, keepdims=True), axis=0, keepdims=True)
        p = jnp.exp(s_all - mx)
        dens.append(jnp.sum(jnp.sum(p, axis=1, keepdims=True), axis=0, keepdims=True))
        acc = vbuf[slot, t * SEL_PAGES] * p[0:1]
        for r in range(1, SEL_PAGES):
            acc = acc + vbuf[slot, t * SEL_PAGES + r] * p[r:r + 1]
        o_cols.append(jnp.sum(acc, axis=1, keepdims=True))
        p_own.append(p[SEL_PAGES:SEL_PAGES + 1, :SAMPLE_ROWS])
    pad = SAMPLE_ROWS - n_tok
    o_sel = jnp.concatenate(o_cols + [jnp.zeros((ATT_HEAD_DIM, pad), F32)], axis=1).T
    o_own = _mm(jnp.concatenate(p_own + [jnp.zeros((pad, SAMPLE_ROWS), F32)], axis=0), vo_ref[0, 0], F32)
    den = jnp.concatenate(dens + [jnp.ones((pad, 1), F32)], axis=0)
    o_ref[0, 0] = (o_sel + o_own) / den


def _sample_attn(page_table_flat, sel_flat, q, k_own, v_own, cache_kt, cache_vt, layer, n_pages, n_tok):
    bn, nh, rows, dh = q.shape
    assert SEL_PAGES + 1 <= SAMPLE_ROWS and rows == SAMPLE_ROWS
    n_buf = n_tok * SEL_PAGES
    own = pl.BlockSpec((1, 1, rows, dh), lambda s, pt, sel: (s // nh, s % nh, 0, 0))
    hbm = pl.BlockSpec(memory_space=pl.ANY)
    return pl.pallas_call(
        functools.partial(_sample_attn_body, layer=layer, n_pages=n_pages, n_tok=n_tok, n_heads=nh),
        grid_spec=pltpu.PrefetchScalarGridSpec(
            num_scalar_prefetch=2,
            grid=(bn * nh,),
            in_specs=[own, own, own, hbm, hbm],
            out_specs=own,
            scratch_shapes=[pltpu.VMEM((2, n_buf, dh, PAGE_SIZE), F32),
                            pltpu.VMEM((2, n_buf, dh, PAGE_SIZE), F32),
                            pltpu.SemaphoreType.DMA((2, 2, n_buf))]),
        out_shape=jax.ShapeDtypeStruct((bn, nh, rows, dh), F32),
        compiler_params=_cparams(("arbitrary",)),
        name="sample_attn",
    )(page_table_flat, sel_flat, q, k_own, v_own, cache_kt, cache_vt)


def _rope_tables(pos):
    half = ATT_HEAD_DIM // 2
    inv = ROPE_THETA ** (-jnp.arange(half, dtype=F32) / half)
    ang = pos.astype(F32)[:, None] * inv[None, :]
    cos = jnp.cos(ang)
    sin = jnp.sin(ang)
    reps = LANES // ATT_HEAD_DIM
    return (jnp.tile(jnp.concatenate([cos, cos], axis=1), (1, reps)),
            jnp.tile(jnp.concatenate([-sin, sin], axis=1), (1, reps)))


def _reorder_w_in(w):
    offs = [0]
    for width in (SSD_D, SSD_CONV_CH, SSD_HEADS, RET_QK, RET_QK, RET_V, RET_V):
        offs.append(offs[-1] + width)
    z, xbc, dt, rq, rk, rv, rg = [w[:, offs[i]:offs[i + 1]] for i in range(7)]
    dt = jnp.pad(dt, ((0, 0), (0, DT_PAD - SSD_HEADS)))
    return jnp.concatenate([z, rv, rg, xbc, rq, rk, dt], axis=1)


def _pad_rows(x, rows):
    return jnp.pad(x, [(0, 0)] * (x.ndim - 2) + [(0, rows - x.shape[-2]), (0, 0)])


def kernel(x_prompt, x_sample, p_prompt, p_sample, state_ssm, state_conv, state_ret, cache_k, cache_v, page_table,
           norm_mix, norm_ffn, norm_ple, ple_proj, ple_gate, hyb_w_in, hyb_w_out, ssd_conv_w, ssd_conv_b,
           ssd_dt_bias, ssd_a_log, ssd_d, ssd_norm, ret_norm, ffn_w_gate, ffn_w_up, ffn_w_down,
           att_w_in, att_w_out, att_q_norm, att_k_norm, moe_router, moe_w_gate, moe_w_up, moe_w_down):
    bp, sp, d = x_prompt.shape
    bs, ts, _ = x_sample.shape
    depth = norm_mix.shape[0]
    n_pages = page_table.shape[1]
    past = n_pages * PAGE_SIZE
    assert sp % MOBA_BLOCK == 0 and past % MOBA_BLOCK == 0 and past >= MOBA_TOPK * MOBA_BLOCK
    assert ts <= SAMPLE_ROWS and math.gcd(ts, SCAN_CHUNK) == ts
    rs = SAMPLE_ROWS

    hp = x_prompt.reshape(bp * sp, d)
    hs = _pad_rows(x_sample, rs).reshape(bs * rs, d)
    pp = p_prompt.reshape(depth, bp * sp, P_DIM)
    ps = _pad_rows(p_sample, rs).reshape(depth, bs * rs, P_DIM)
    cos_p, sin_p = _rope_tables(jnp.arange(sp))
    cos_s, sin_s = _rope_tables(past + jnp.arange(rs))
    pt_flat = page_table.reshape(-1)
    cache_kt = jnp.swapaxes(cache_k, 3, 4)
    cache_vt = jnp.swapaxes(cache_v, 3, 4)
    page_means = _page_mean(cache_kt, 8)

    tm_p, tm_s = 512, bs * rs
    n_chunks = sp // SCAN_CHUNK
    outs = {k: [] for k in ("ssm_p", "conv_p", "ret_p", "k_p", "v_p", "ssm_s", "conv_s", "ret_s", "k_s", "v_s")}
    for i in range(depth):
        j = i // 2
        if i % 2 == 0:
            w_in = _reorder_w_in(hyb_w_in[j])
            w_out = hyb_w_out[j]
            wts = (ssd_conv_w[j], ssd_conv_b[j], ssd_dt_bias[j], ssd_a_log[j], ssd_d[j], ssd_norm[j], ret_norm[j])
            front = SUBLANES - (SSD_CONV - 1)
            u = _norm_matmul(hp, norm_mix[i], w_in.astype(BF16), tm_p, U_WIDTH // 5)
            mix, ssm, conv, ret = _mixer(
                u, cos_p, sin_p,
                jnp.zeros((bp, SUBLANES, SSD_CONV_CH), F32),
                jnp.zeros((bp, SSD_HEADS, SSD_STATE, SSD_HEAD_DIM), F32),
                jnp.zeros((bp, RET_HEADS, RET_DK, RET_DV), F32), *wts,
                n_seq=bp, n_chunks=n_chunks, q_len=SCAN_CHUNK, n_valid=SCAN_CHUNK, pos_per_chunk=True,
                mxu=BF16)
            hp = _matmul_res(mix, w_out.astype(BF16), hp, tm_p)
            outs["ssm_p"].append(ssm); outs["conv_p"].append(conv[:, front:]); outs["ret_p"].append(ret)
            u = _norm_matmul(hs, norm_mix[i], w_in, tm_s, U_WIDTH // 5)
            mix, ssm, conv, ret = _mixer(
                u, cos_s, sin_s, jnp.pad(state_conv[j], ((0, 0), (front, 0), (0, 0))), state_ssm[j], state_ret[j],
                *wts, n_seq=bs, n_chunks=1, q_len=rs, n_valid=ts, pos_per_chunk=False, mxu=F32)
            hs = _matmul_res(mix, w_out, hs, tm_s)
            outs["ssm_s"].append(ssm); outs["conv_s"].append(conv[:, front:]); outs["ret_s"].append(ret)
            wg, wu, wd = ffn_w_gate[j][None], ffn_w_up[j][None], ffn_w_down[j][None]
            tf = wg.shape[2] // 2
            hp = _ffn(hp, norm_ffn[i], wg.astype(BF16), wu.astype(BF16), wd.astype(BF16), None, tm_p, tf)
            hs = _ffn(hs, norm_ffn[i], wg, wu, wd, None, tm_s, tf)
        else:
            w_in = att_w_in[j]
            w_out = att_w_out[j]
            qkv = _norm_matmul(hp, norm_mix[i], w_in.astype(BF16), tm_p, ATT_D)
            q, k, v = _qkv_post(qkv, att_q_norm[j], att_k_norm[j], cos_p, sin_p, bp, sp, 256, True)
            hp = _matmul_res(_moba_prompt(q, k, v).reshape(bp * sp, ATT_D), w_out.astype(BF16), hp, tm_p)
            outs["k_p"].append(k); outs["v_p"].append(v)
            qkv = _norm_matmul(hs, norm_mix[i], w_in, tm_s, ATT_D)
            q, k, v = _qkv_post(qkv, att_q_norm[j], att_k_norm[j], cos_s, sin_s, bs, rs, rs, False)
            sel = _sample_select(pt_flat, q, page_means, j, n_pages)
            sel_flat = sel[:, :, :ts, :MOBA_TOPK].reshape(-1)
            o = _sample_attn(pt_flat, sel_flat, q, k, v, cache_kt, cache_vt, j, n_pages, ts)
            o = o.transpose(0, 2, 1, 3).reshape(bs * rs, ATT_D)
            hs = _matmul_res(o, w_out, hs, tm_s)
            outs["k_s"].append(k[:, :, :ts]); outs["v_s"].append(v[:, :, :ts])
            wg, wu, wd = moe_w_gate[j], moe_w_up[j], moe_w_down[j]
            hp = _ffn(hp, norm_ffn[i], wg.astype(BF16), wu.astype(BF16), wd.astype(BF16), moe_router[j], tm_p,
                      wg.shape[2])
            hs = _ffn(hs, norm_ffn[i], wg, wu, wd, moe_router[j], tm_s, wg.shape[2])
        hp = _ple(hp, pp, i, norm_ple[i], ple_gate[i].astype(BF16), ple_proj[i].astype(BF16), tm_p)
        hs = _ple(hs, ps, i, norm_ple[i], ple_gate[i], ple_proj[i], tm_s)

    st = lambda name: jnp.stack(outs[name])
    return (hp.reshape(bp, sp, d), hs.reshape(bs, rs, d)[:, :ts],
            st("ssm_p"), st("conv_p"), st("ret_p"), st("k_p"), st("v_p"),
            st("ssm_s"), st("conv_s"), st("ret_s"), st("k_s"), st("v_s"))
```

```python
import functools
import math

import jax
import jax.numpy as jnp
from jax import lax
from jax.experimental import pallas as pl
from jax.experimental.pallas import tpu as pltpu

F32 = jnp.float32
BF16 = jnp.bfloat16
HIGHEST = lax.Precision.HIGHEST

D_MODEL = 1024
P_DIM = 256
EPS = 1e-6
ROPE_THETA = 10000.0
PAGE_SIZE = 128
SSD_HEADS = 16
SSD_HEAD_DIM = 64
SSD_D = SSD_HEADS * SSD_HEAD_DIM
SSD_GROUPS = 2
SSD_STATE = 128
SSD_CONV = 4
SSD_CONV_CH = SSD_D + 2 * SSD_GROUPS * SSD_STATE
RET_HEADS = 8
RET_DK = 64
RET_DV = 128
RET_QK = RET_HEADS * RET_DK
RET_V = RET_HEADS * RET_DV
MIX_WIDTH = SSD_D + RET_V
SCAN_CHUNK = 128
ATT_HEADS = 16
ATT_HEAD_DIM = 64
ATT_D = ATT_HEADS * ATT_HEAD_DIM
MOBA_BLOCK = 256
MOBA_TOPK = 3
N_EXPERTS = 8
NEG = -1e30

LANES = 128
SUBLANES = 8
SAMPLE_ROWS = 8
DT_PAD = LANES
U_Z, U_RV, U_RG, U_XBC, U_RQ, U_RK, U_DT = 0, 1024, 2048, 3072, 4608, 5120, 5632
U_WIDTH = U_DT + DT_PAD
VMEM_LIMIT = 56 * 1024 * 1024


def _cparams(sem):
    return pltpu.CompilerParams(dimension_semantics=sem, vmem_limit_bytes=VMEM_LIMIT)


def _rms(x, g):
    return x * lax.rsqrt(jnp.mean(x * x, axis=-1, keepdims=True) + EPS) * g


def _silu(x):
    return x * jax.nn.sigmoid(x)


NN = (((1,), (0,)), ((), ()))
NT = (((1,), (1,)), ((), ()))
TN = (((0,), (0,)), ((), ()))


def _mm(a, b, mxu, dims=NN):
    return lax.dot_general(a.astype(mxu), b.astype(mxu), dims, preferred_element_type=F32,
                           precision=HIGHEST if mxu == F32 else None)


def _norm_matmul_body(x_ref, g_ref, w_ref, o_ref, xn_ref):
    @pl.when(pl.program_id(1) == 0)
    def _():
        xn_ref[...] = _rms(x_ref[...], g_ref[...]).astype(xn_ref.dtype)

    o_ref[...] = _mm(xn_ref[...], w_ref[...], w_ref.dtype)


def _norm_matmul(x, g, w, tm, tn):
    m, d = x.shape
    n = w.shape[1]
    return pl.pallas_call(
        _norm_matmul_body,
        grid=(m // tm, n // tn),
        in_specs=[pl.BlockSpec((tm, d), lambda i, j: (i, 0)),
                  pl.BlockSpec((1, d), lambda i, j: (0, 0)),
                  pl.BlockSpec((d, tn), lambda i, j: (0, j))],
        out_specs=pl.BlockSpec((tm, tn), lambda i, j: (i, j)),
        out_shape=jax.ShapeDtypeStruct((m, n), F32),
        scratch_shapes=[pltpu.VMEM((tm, d), w.dtype)],
        compiler_params=_cparams(("parallel", "arbitrary")),
        name="norm_matmul",
    )(x, g.reshape(1, d), w)


def _matmul_res_body(a_ref, w_ref, r_ref, o_ref):
    o_ref[...] = r_ref[...] + _mm(a_ref[...], w_ref[...], w_ref.dtype)


def _matmul_res(a, w, res, tm):
    m, k = a.shape
    n = w.shape[1]
    return pl.pallas_call(
        _matmul_res_body,
        grid=(m // tm,),
        in_specs=[pl.BlockSpec((tm, k), lambda i: (i, 0)),
                  pl.BlockSpec((k, n), lambda i: (0, 0)),
                  pl.BlockSpec((tm, n), lambda i: (i, 0))],
        out_specs=pl.BlockSpec((tm, n), lambda i: (i, 0)),
        out_shape=jax.ShapeDtypeStruct((m, n), F32),
        compiler_params=_cparams(("parallel",)),
        name="matmul_res",
    )(a, w, res)


def _rope_lanes(x, cos, sin_signed):
    lane = lax.broadcasted_iota(jnp.int32, x.shape, 1)
    rot = jnp.where(lane % ATT_HEAD_DIM < ATT_HEAD_DIM // 2,
                    pltpu.roll(x, LANES - ATT_HEAD_DIM // 2, 1),
                    pltpu.roll(x, ATT_HEAD_DIM // 2, 1))
    return x * cos + rot * sin_signed


def _mixer_body(z_ref, rv_ref, rg_ref, xbc_ref, rq_ref, rk_ref, dt_ref, cos_ref, sin_ref,
                conv0_ref, ssm0_ref, ret0_ref, convw_ref, convb_ref, dtb_ref, alog_ref,
                dskip_ref, ssdg_ref, retg_ref,
                mix_ref, ssm_ref, conv_ref, ret_ref, *, q_len, n_valid, mxu):
    c = pl.program_id(1)

    @pl.when(c == 0)
    def _():
        ssm_ref[...] = ssm0_ref[...]
        ret_ref[...] = ret0_ref[...]
        conv_ref[...] = conv0_ref[...]

    row = lax.broadcasted_iota(jnp.int32, (q_len, 1), 0)
    valid = row < n_valid
    qi = lax.broadcasted_iota(jnp.int32, (q_len, q_len), 0)
    ki = lax.broadcasted_iota(jnp.int32, (q_len, q_len), 1)
    causal = ki <= qi

    xbc = xbc_ref[...]
    ext = jnp.concatenate([conv_ref[0], xbc], axis=0)
    acc = convb_ref[...]
    for j in range(SSD_CONV):
        lo = SUBLANES - (SSD_CONV - 1) + j
        acc = acc + ext[lo:lo + q_len] * convw_ref[j:j + 1, :]
    conv_ref[0] = ext[n_valid:n_valid + SUBLANES]
    xc = _silu(acc)
    xs = xc[:, :SSD_D]
    bs = xc[:, SSD_D:SSD_D + SSD_GROUPS * SSD_STATE]
    cs = xc[:, SSD_D + SSD_GROUPS * SSD_STATE:]

    dt_in = dt_ref[...] + dtb_ref[...]
    dt = jnp.maximum(dt_in, 0.0) + jnp.log1p(jnp.exp(-jnp.abs(dt_in)))
    dt = jnp.where(valid, dt, 0.0)
    la = dt * (-jnp.exp(alog_ref[...]))
    cum = jnp.dot(causal.astype(F32), la, preferred_element_type=F32, precision=HIGHEST)
    cum_t = cum.T
    dt_t = dt.T
    last = cum[q_len - 1:q_len, :]
    e_cum = jnp.exp(cum)
    w_all = jnp.exp(last - cum) * dt
    e_last = jnp.exp(last)
    rep = SSD_HEADS // SSD_GROUPS
    ys = []
    for g in range(SSD_GROUPS):
        c_g = cs[:, g * SSD_STATE:(g + 1) * SSD_STATE].astype(mxu)
        b_g = bs[:, g * SSD_STATE:(g + 1) * SSD_STATE].astype(mxu)
        cb = _mm(c_g, b_g, mxu, NT)
        for h in range(g * rep, (g + 1) * rep):
            diff = cum[:, h:h + 1] - cum_t[h:h + 1, :]
            m = cb * jnp.exp(jnp.where(causal, diff, -jnp.inf)) * dt_t[h:h + 1, :]
            x_h = xs[:, h * SSD_HEAD_DIM:(h + 1) * SSD_HEAD_DIM]
            s_in = ssm_ref[0, h]
            y_h = _mm(m, x_h, mxu) + e_cum[:, h:h + 1] * _mm(c_g, s_in, mxu)
            ssm_ref[0, h] = e_last[:, h:h + 1] * s_in + _mm(b_g, x_h * w_all[:, h:h + 1], mxu, TN)
            ys.append(y_h)
    y = jnp.concatenate(ys, axis=1) + dskip_ref[...] * xs
    y_ssd = _rms(y * _silu(z_ref[...]), ssdg_ref[...])

    cos = cos_ref[...]
    sin = sin_ref[...]
    cnt = jnp.minimum(row + 1, n_valid).astype(F32)
    cnt_t = jnp.minimum(ki[0:1, :] + 1, n_valid).astype(F32)
    yrs = []
    for hp in range(RET_HEADS // 2):
        sl = slice(hp * LANES, (hp + 1) * LANES)
        q2 = _rope_lanes(rq_ref[:, sl], cos, sin)
        k2 = _rope_lanes(rk_ref[:, sl], cos, sin) * (RET_DK ** -0.5)
        for h in (2 * hp, 2 * hp + 1):
            log_g = math.log1p(-(2.0 ** (-5.0 - h)))
            o = (h % 2) * RET_DK
            q_h = q2[:, o:o + RET_DK]
            k_h = k2[:, o:o + RET_DK]
            v_h = rv_ref[:, h * RET_DV:(h + 1) * RET_DV]
            dec = jnp.exp(jnp.where(causal, (cnt - cnt_t) * log_g, -jnp.inf))
            s_in = ret_ref[0, h]
            y_h = _mm(_mm(q_h, k_h, mxu, NT) * dec, v_h, mxu) + jnp.exp(cnt * log_g) * _mm(q_h, s_in, mxu)
            w_k = jnp.where(valid, jnp.exp((n_valid - cnt) * log_g), 0.0)
            ret_ref[0, h] = math.exp(n_valid * log_g) * s_in + _mm(k_h, v_h * w_k, mxu, TN)
            yrs.append(y_h * lax.rsqrt(jnp.mean(y_h * y_h, axis=-1, keepdims=True) + EPS))
    yr = jnp.concatenate(yrs, axis=1) * retg_ref[...] * _silu(rg_ref[...])
    mix_ref[...] = jnp.concatenate([y_ssd, yr], axis=1).astype(mix_ref.dtype)


def _mixer(u, cos, sin, conv0, ssm0, ret0, conv_w, conv_b, dt_bias, a_log, d_skip, ssd_g, ret_g,
           *, n_seq, n_chunks, q_len, n_valid, pos_per_chunk, mxu):
    def ub(width, off):
        return pl.BlockSpec((q_len, width), lambda b, c: (b * n_chunks + c, off // width))

    def full2(a):
        return pl.BlockSpec(a.shape, lambda b, c: (0, 0))

    pos_spec = pl.BlockSpec((q_len, LANES), (lambda b, c: (c, 0)) if pos_per_chunk else (lambda b, c: (0, 0)))
    state = lambda a: pl.BlockSpec((1,) + a.shape[1:], lambda b, c: (b,) + (0,) * (a.ndim - 1))
    pad16 = lambda v: jnp.pad(v.reshape(1, -1), ((0, 0), (0, DT_PAD - SSD_HEADS)))
    params = [conv_w, conv_b.reshape(1, -1), pad16(dt_bias), pad16(a_log),
              jnp.repeat(d_skip, SSD_HEAD_DIM).reshape(1, -1), ssd_g.reshape(1, -1), ret_g.reshape(1, -1)]
    rows = n_seq * n_chunks * q_len
    out_shape = (jax.ShapeDtypeStruct((rows, MIX_WIDTH), mxu),
                 jax.ShapeDtypeStruct(ssm0.shape, F32),
                 jax.ShapeDtypeStruct(conv0.shape, F32),
                 jax.ShapeDtypeStruct(ret0.shape, F32))
    return pl.pallas_call(
        functools.partial(_mixer_body, q_len=q_len, n_valid=n_valid, mxu=mxu),
        grid=(n_seq, n_chunks),
        in_specs=[ub(SSD_D, U_Z), ub(RET_V, U_RV), ub(RET_V, U_RG), ub(SSD_CONV_CH, U_XBC),
                  ub(RET_QK, U_RQ), ub(RET_QK, U_RK), ub(DT_PAD, U_DT), pos_spec, pos_spec,
                  state(conv0), state(ssm0), state(ret0)] + [full2(p) for p in params],
        out_specs=(pl.BlockSpec((q_len, MIX_WIDTH), lambda b, c: (b * n_chunks + c, 0)),
                   state(ssm0), state(conv0), state(ret0)),
        out_shape=out_shape,
        compiler_params=_cparams(("parallel", "arbitrary")),
        name="mixer",
    )(u, u, u, u, u, u, u, cos, sin, conv0, ssm0, ret0, *params)


def _route(xn, router):
    logits = jnp.dot(xn, router, preferred_element_type=F32, precision=HIGHEST)
    lane = lax.broadcasted_iota(jnp.int32, logits.shape, 1)
    lg = jnp.where(lane < N_EXPERTS, logits, -jnp.inf)
    m1 = jnp.max(lg, axis=1, keepdims=True)
    i1 = jnp.min(jnp.where(lg == m1, lane, LANES), axis=1, keepdims=True)
    lg2 = jnp.where(lane == i1, -jnp.inf, lg)
    m2 = jnp.max(lg2, axis=1, keepdims=True)
    i2 = jnp.min(jnp.where(lg2 == m2, lane, LANES), axis=1, keepdims=True)
    e2 = jnp.exp(m2 - m1)
    den = 1.0 + e2
    return jnp.where(lane == i1, 1.0 / den, 0.0) + jnp.where(lane == i2, e2 / den, 0.0)


def _ffn_body(*refs, routed):
    if routed:
        x_ref, g_ref, r_ref, wg_ref, wu_ref, wd_ref, o_ref, xn_ref, acc_ref, comb_ref = refs
    else:
        x_ref, g_ref, wg_ref, wu_ref, wd_ref, o_ref, xn_ref, acc_ref = refs
    e = pl.program_id(1)
    f = pl.program_id(2)

    @pl.when((e == 0) & (f == 0))
    def _():
        xn = _rms(x_ref[...], g_ref[...])
        xn_ref[...] = xn.astype(xn_ref.dtype)
        acc_ref[...] = jnp.zeros_like(acc_ref)
        if routed:
            comb_ref[...] = _route(xn, r_ref[...])

    xn = xn_ref[...]
    mxu = wg_ref.dtype
    a = _silu(_mm(xn, wg_ref[0], mxu)) * _mm(xn, wu_ref[0], mxu)
    y = _mm(a, wd_ref[0], mxu)
    if routed:
        lane = lax.broadcasted_iota(jnp.int32, comb_ref.shape, 1)
        y = y * jnp.sum(jnp.where(lane == e, comb_ref[...], 0.0), axis=1, keepdims=True)
    acc_ref[...] += y

    @pl.when((e == pl.num_programs(1) - 1) & (f == pl.num_programs(2) - 1))
    def _():
        o_ref[...] = x_ref[...] + acc_ref[...]


def _ffn(x, g, wg, wu, wd, router, tm, tf):
    m, d = x.shape
    n_e, _, f_dim = wg.shape
    routed = router is not None
    in_specs = [pl.BlockSpec((tm, d), lambda i, e, f: (i, 0)),
                pl.BlockSpec((1, d), lambda i, e, f: (0, 0))]
    args = [x, g.reshape(1, d)]
    scratch = [pltpu.VMEM((tm, d), wg.dtype), pltpu.VMEM((tm, d), F32)]
    if routed:
        in_specs.append(pl.BlockSpec((d, LANES), lambda i, e, f: (0, 0)))
        args.append(jnp.pad(router, ((0, 0), (0, LANES - n_e))))
        scratch.append(pltpu.VMEM((tm, LANES), F32))
    in_specs += [pl.BlockSpec((1, d, tf), lambda i, e, f: (e, 0, f)),
                 pl.BlockSpec((1, d, tf), lambda i, e, f: (e, 0, f)),
                 pl.BlockSpec((1, tf, d), lambda i, e, f: (e, f, 0))]
    return pl.pallas_call(
        functools.partial(_ffn_body, routed=routed),
        grid=(m // tm, n_e, f_dim // tf),
        in_specs=in_specs,
        out_specs=pl.BlockSpec((tm, d), lambda i, e, f: (i, 0)),
        out_shape=jax.ShapeDtypeStruct((m, d), F32),
        scratch_shapes=scratch,
        compiler_params=_cparams(("parallel", "arbitrary", "arbitrary")),
        name="moe" if routed else "ffn",
    )(*args, wg, wu, wd)


def _ple_body(x_ref, p_ref, g_ref, wg_ref, wp_ref, o_ref):
    x = x_ref[...]
    mxu = wg_ref.dtype
    gate = jax.nn.sigmoid(_mm(_rms(x, g_ref[...]), wg_ref[...], mxu))
    o_ref[...] = x + gate * _mm(p_ref[...], wp_ref[...], mxu)


def _ple(x, p_all, layer, g, w_gate, w_proj, tm):
    m, d = x.shape
    return pl.pallas_call(
        _ple_body,
        grid=(m // tm,),
        in_specs=[pl.BlockSpec((tm, d), lambda i: (i, 0)),
                  pl.BlockSpec((None, tm, P_DIM), lambda i: (layer, i, 0)),
                  pl.BlockSpec((1, d), lambda i: (0, 0)),
                  pl.BlockSpec((d, d), lambda i: (0, 0)),
                  pl.BlockSpec((P_DIM, d), lambda i: (0, 0))],
        out_specs=pl.BlockSpec((tm, d), lambda i: (i, 0)),
        out_shape=jax.ShapeDtypeStruct((m, d), F32),
        compiler_params=_cparams(("parallel",)),
        name="ple",
    )(x, p_all, g.reshape(1, d), w_gate, w_proj)


def _qkv_post_body(qkv_ref, qn_ref, kn_ref, cos_ref, sin_ref, q_ref, k_ref, v_ref):
    half = ATT_HEAD_DIM // 2
    cos = cos_ref[:, :ATT_HEAD_DIM]
    sin = sin_ref[:, :ATT_HEAD_DIM]

    def norm_rope(x, gain):
        y = x * lax.rsqrt(jnp.mean(x * x, axis=-1, keepdims=True) + EPS) * gain
        rot = jnp.concatenate([y[:, half:], y[:, :half]], axis=1)
        return y * cos + rot * sin

    for h in range(ATT_HEADS):
        sl = slice(h * ATT_HEAD_DIM, (h + 1) * ATT_HEAD_DIM)
        q_ref[0, h] = norm_rope(qkv_ref[:, sl], qn_ref[...]) * (ATT_HEAD_DIM ** -0.5)
        k_ref[0, h] = norm_rope(qkv_ref[:, ATT_D + h * ATT_HEAD_DIM:ATT_D + (h + 1) * ATT_HEAD_DIM],
                                kn_ref[...])
        v_ref[0, h] = qkv_ref[:, 2 * ATT_D + h * ATT_HEAD_DIM:2 * ATT_D + (h + 1) * ATT_HEAD_DIM]


def _qkv_post(qkv, qn, kn, cos, sin, n_seq, seq, ts, pos_per_tile):
    nt = seq // ts
    hm = jax.ShapeDtypeStruct((n_seq, ATT_HEADS, seq, ATT_HEAD_DIM), F32)
    hm_spec = pl.BlockSpec((1, ATT_HEADS, ts, ATT_HEAD_DIM), lambda b, t: (b, 0, t, 0))
    pos_spec = pl.BlockSpec((ts, LANES), (lambda b, t: (t, 0)) if pos_per_tile else (lambda b, t: (0, 0)))
    vec = pl.BlockSpec((1, ATT_HEAD_DIM), lambda b, t: (0, 0))
    return pl.pallas_call(
        _qkv_post_body,
        grid=(n_seq, nt),
        in_specs=[pl.BlockSpec((ts, 3 * ATT_D), lambda b, t: (b * nt + t, 0)), vec, vec, pos_spec, pos_spec],
        out_specs=(hm_spec, hm_spec, hm_spec),
        out_shape=(hm, hm, hm),
        compiler_params=_cparams(("parallel", "parallel")),
        name="qkv_post",
    )(qkv, qn.reshape(1, -1), kn.reshape(1, -1), cos, sin)


HEADS_PER_STEP = LANES // ATT_HEAD_DIM


def _moba_prompt_body(q_ref, k_ref, v_ref, o_ref, kb_ref, vb_ref, km_ref, qa_ref, *, n_blocks):
    qi = pl.program_id(2)
    blk = MOBA_BLOCK
    dh = ATT_HEAD_DIM
    tail = LANES - dh
    nb8 = -(-n_blocks // SUBLANES) * SUBLANES
    assert nb8 + SUBLANES <= tail
    shift_slack = 60.0

    @pl.when(qi == 0)
    def _():
        s_len = n_blocks * blk
        km_ref[...] = jnp.zeros_like(km_ref)
        r_blk = lax.broadcasted_iota(jnp.int32, (s_len, tail), 0) // blk
        c_idx = lax.broadcasted_iota(jnp.int32, (s_len, tail), 1)
        k_tail = ((r_blk == c_idx) | (c_idx == nb8)).astype(BF16)
        v_tail = (c_idx == 0).astype(BF16)
        blk_i = lax.broadcasted_iota(jnp.int32, (nb8, s_len), 0)
        blk_f = blk_i.astype(F32)
        past = blk_i < lax.broadcasted_iota(jnp.int32, (nb8, s_len), 1) // blk
        row8 = lax.broadcasted_iota(jnp.int32, (SUBLANES, s_len), 0)
        for hh in range(HEADS_PER_STEP):
            kb = k_ref[0, hh].astype(BF16)
            kb_ref[hh] = jnp.concatenate([kb, k_tail], axis=1)
            vb_ref[hh] = jnp.concatenate([v_ref[0, hh].astype(BF16), v_tail], axis=1)
            kf = kb.astype(F32)
            k_max = jnp.sqrt(jnp.max(jnp.sum(kf * kf, axis=1, keepdims=True), axis=0, keepdims=True))
            for n in range(n_blocks):
                km_ref[hh, n:n + 1, :] = jnp.mean(k_ref[0, hh, n * blk:(n + 1) * blk, :], axis=0, keepdims=True)

            q = q_ref[0, hh]
            qb = q.astype(BF16)
            gate = lax.dot_general(km_ref[hh, :nb8, :], q, NT, preferred_element_type=F32, precision=HIGHEST)
            g = jnp.where(past, gate, NEG)
            sel = jnp.zeros((nb8, s_len), jnp.bool_)
            for _ in range(MOBA_TOPK):
                mx = jnp.max(g, axis=0, keepdims=True)
                idx = jnp.min(jnp.where(g == mx, blk_f, float(LANES)), axis=0, keepdims=True)
                pick = blk_f == idx
                sel = sel | (pick & past)
                g = jnp.where(pick, -jnp.inf, g)
            qf = qb.astype(F32)
            q_sq = lax.dot_general(jnp.ones((SUBLANES, dh), F32), qf * qf, NT, preferred_element_type=F32,
                                   precision=HIGHEST)
            shift = 1.01 * jnp.sqrt(q_sq) * k_max
            cols_t = jnp.concatenate([jnp.where(sel, 0.0, NEG), jnp.where(row8 == 0, -shift, 0.0),
                                      jnp.zeros((tail - nb8 - SUBLANES, s_len), F32)], axis=0)
            for n in range(n_blocks):
                rows = slice(n * blk, (n + 1) * blk)
                qa_ref[hh, rows, :] = jnp.concatenate([qb[rows], cols_t[:, rows].T.astype(BF16)], axis=1)

    causal = (lax.broadcasted_iota(jnp.int32, (blk, blk), 1) <= lax.broadcasted_iota(jnp.int32, (blk, blk), 0))
    own = pl.multiple_of(qi * blk, blk)
    lane = lax.broadcasted_iota(jnp.int32, (blk, LANES), 1)
    mask_lanes = (lane >= dh) & (lane < dh + nb8)
    q_aug, q_own, acc0s = [], [], []
    lowest = None
    for hh in range(HEADS_PER_STEP):
        q_aug.append(qa_ref[hh, pl.ds(own, blk), :])
        q_own.append(jnp.where(mask_lanes, jnp.zeros_like(q_aug[hh]), q_aug[hh]))

        s = jnp.where(causal, _mm(q_own[hh], kb_ref[hh, pl.ds(own, blk), :], BF16, NT), NEG)
        acc0s.append(_mm(jnp.exp(s), vb_ref[hh, pl.ds(own, blk), :], BF16))
        low = jnp.min(jnp.max(s, axis=1, keepdims=True), axis=0, keepdims=True)
        lowest = low if lowest is None else jnp.minimum(lowest, low)

    def shifted(_):
        def add_keys(off, width, accs):
            return tuple(
                accs[hh] + _mm(jnp.exp(_mm(q_aug[hh], kb_ref[hh, pl.ds(off, width), :], BF16, NT)),
                               vb_ref[hh, pl.ds(off, width), :], BF16)
                for hh in range(HEADS_PER_STEP))

        accs = lax.fori_loop(0, qi // 2, lambda i, a: add_keys(pl.multiple_of(i * 2 * blk, 2 * blk), 2 * blk, a),
                             tuple(acc0s))
        return lax.cond(qi % 2 == 1, lambda a: add_keys(pl.multiple_of((qi - 1) * blk, blk), blk, a),
                        lambda a: a, accs)

    def running_max(_):
        init = []
        for hh in range(HEADS_PER_STEP):
            s = jnp.where(causal, _mm(q_own[hh], kb_ref[hh, pl.ds(own, blk), :], BF16, NT), NEG)
            m0 = jnp.max(s, axis=1, keepdims=True)
            init += [m0, _mm(jnp.exp(s - m0), vb_ref[hh, pl.ds(own, blk), :], BF16)]

        def body(n, carry):
            off = pl.multiple_of(n * blk, blk)
            new = []
            for hh in range(HEADS_PER_STEP):
                m_i, acc = carry[2 * hh], carry[2 * hh + 1]
                s = _mm(q_aug[hh], kb_ref[hh, pl.ds(off, blk), :], BF16, NT)
                m_new = jnp.maximum(m_i, jnp.max(s, axis=1, keepdims=True))
                acc = jnp.exp(m_i - m_new) * acc + _mm(jnp.exp(s - m_new), vb_ref[hh, pl.ds(off, blk), :], BF16)
                new += [m_new, acc]
            return tuple(new)

        fin = lax.fori_loop(0, qi, body, tuple(init))
        return tuple(fin[2 * hh + 1] for hh in range(HEADS_PER_STEP))

    accs = lax.cond(lowest[0, 0] < -shift_slack, running_max, shifted, None)
    outs = [a[:, :dh] / a[:, dh:dh + 1] for a in accs]
    o_ref[0] = jnp.concatenate(outs, axis=1).astype(BF16)


def _moba_prompt(q, k, v):
    bn, nh, s, dh = q.shape
    n_blocks = s // MOBA_BLOCK
    hps = HEADS_PER_STEP
    return pl.pallas_call(
        functools.partial(_moba_prompt_body, n_blocks=n_blocks),
        grid=(bn, nh // hps, n_blocks),
        in_specs=[pl.BlockSpec((1, hps, s, dh), lambda b, h, i: (b, h, 0, 0)),
                  pl.BlockSpec((1, hps, s, dh), lambda b, h, i: (b, h, 0, 0)),
                  pl.BlockSpec((1, hps, s, dh), lambda b, h, i: (b, h, 0, 0))],
        out_specs=pl.BlockSpec((1, MOBA_BLOCK, hps * dh), lambda b, h, i: (b, i, h)),
        out_shape=jax.ShapeDtypeStruct((bn, s, nh * dh), BF16),
        scratch_shapes=[pltpu.VMEM((hps, s, LANES), BF16), pltpu.VMEM((hps, s, LANES), BF16),
                        pltpu.VMEM((hps, LANES, dh), F32), pltpu.VMEM((hps, s, LANES), BF16)],
        compiler_params=_cparams(("parallel", "parallel", "arbitrary")),
        name="moba_prompt",
    )(q, k, v)


def _page_mean_body(k_ref, o_ref, *, pages_per_step):
    row = lax.broadcasted_iota(jnp.int32, (ATT_HEAD_DIM, LANES), 0)
    lane = lax.broadcasted_iota(jnp.int32, (ATT_HEAD_DIM, LANES), 1)
    for p in range(pages_per_step):
        for hp in range(ATT_HEADS // HEADS_PER_STEP):
            diag = jnp.zeros((ATT_HEAD_DIM, LANES), F32)
            for k in range(HEADS_PER_STEP):
                sums = jnp.sum(k_ref[0, p, hp * HEADS_PER_STEP + k], axis=1, keepdims=True)
                diag = diag + jnp.where(lane == row + k * ATT_HEAD_DIM, sums, 0.0)
            o_ref[0, p:p + 1, hp * LANES:(hp + 1) * LANES] = (
                jnp.sum(diag, axis=0, keepdims=True) * (1.0 / PAGE_SIZE))


def _page_mean(cache_kt, pages_per_step):
    nl, n_pool, nh, dh, ps = cache_kt.shape
    assert n_pool % pages_per_step == 0
    return pl.pallas_call(
        functools.partial(_page_mean_body, pages_per_step=pages_per_step),
        grid=(nl, n_pool // pages_per_step),
        in_specs=[pl.BlockSpec((1, pages_per_step, nh, dh, ps), lambda l, i: (l, i, 0, 0, 0))],
        out_specs=pl.BlockSpec((1, pages_per_step, nh * dh), lambda l, i: (l, i, 0)),
        out_shape=jax.ShapeDtypeStruct((nl, n_pool, nh * dh), F32),
        compiler_params=_cparams(("parallel", "parallel")),
        name="page_mean",
    )(cache_kt)


def _sample_select_body(pt_ref, q_ref, pm_ref, sel_ref, gath_ref, *, n_pages):
    b = pl.program_id(0)
    n_blk = n_pages // 2
    for r in range(n_pages):
        logical = 2 * r if r < n_blk else 2 * (r - n_blk) + 1
        page = pt_ref[b * n_pages + logical]
        gath_ref[r:r + 1, :] = pm_ref[0, pl.ds(page, 1), :]
    lane = lax.broadcasted_iota(jnp.int32, (SAMPLE_ROWS, LANES), 1)
    for h in range(ATT_HEADS):
        pm_h = gath_ref[:, h * ATT_HEAD_DIM:(h + 1) * ATT_HEAD_DIM]
        gp = lax.dot_general(q_ref[0, h], pm_h, (((1,), (1,)), ((), ())),
                             preferred_element_type=F32, precision=HIGHEST)
        gate = 0.5 * (gp[:, :n_blk] + gp[:, n_blk:])
        g = jnp.concatenate([gate, jnp.full((SAMPLE_ROWS, LANES - n_blk), -jnp.inf, F32)], axis=1)
        out = jnp.zeros((SAMPLE_ROWS, LANES), jnp.int32)
        for j in range(MOBA_TOPK):
            mx = jnp.max(g, axis=1, keepdims=True)
            idx = jnp.min(jnp.where(g == mx, lane, LANES), axis=1, keepdims=True)
            out = jnp.where(lane == j, idx, out)
            g = jnp.where(lane == idx, -jnp.inf, g)
        sel_ref[0, h] = out


def _sample_select(page_table_flat, q, page_means, layer, n_pages):
    bn, nh, rows, dh = q.shape
    n_pool = page_means.shape[1]
    return pl.pallas_call(
        functools.partial(_sample_select_body, n_pages=n_pages),
        grid_spec=pltpu.PrefetchScalarGridSpec(
            num_scalar_prefetch=1,
            grid=(bn,),
            in_specs=[pl.BlockSpec((1, nh, rows, dh), lambda b, pt: (b, 0, 0, 0)),
                      pl.BlockSpec((1, n_pool, nh * dh), lambda b, pt: (layer, 0, 0))],
            out_specs=pl.BlockSpec((1, nh, rows, LANES), lambda b, pt: (b, 0, 0, 0)),
            scratch_shapes=[pltpu.VMEM((n_pages, nh * dh), F32)]),
        out_shape=jax.ShapeDtypeStruct((bn, nh, rows, LANES), jnp.int32),
        compiler_params=_cparams(("arbitrary",)),
        name="sample_select",
    )(page_table_flat, q, page_means)


PAGES_PER_BLOCK = MOBA_BLOCK // PAGE_SIZE
SEL_PAGES = MOBA_TOPK * PAGES_PER_BLOCK


def _sample_attn_body(pt_ref, sel_ref, q_ref, ko_ref, vo_ref, ck_hbm, cv_hbm, o_ref, kbuf, vbuf, sem,
                      *, layer, n_pages, n_tok, n_heads):
    step = pl.program_id(0)

    def page_copies(st, slot):
        b = st // n_heads
        h = st % n_heads
        out = []
        for t in range(n_tok):
            for j in range(MOBA_TOPK):
                blk = sel_ref[((b * n_heads + h) * n_tok + t) * MOBA_TOPK + j]
                for i in range(PAGES_PER_BLOCK):
                    page = pt_ref[b * n_pages + blk * PAGES_PER_BLOCK + i]
                    r = t * SEL_PAGES + j * PAGES_PER_BLOCK + i
                    out.append(pltpu.make_async_copy(ck_hbm.at[layer, page, h], kbuf.at[slot, r], sem.at[slot, 0, r]))
                    out.append(pltpu.make_async_copy(cv_hbm.at[layer, page, h], vbuf.at[slot, r], sem.at[slot, 1, r]))
        return out

    @pl.when(step == 0)
    def _():
        for cp in page_copies(step, 0):
            cp.start()

    @pl.when(step + 1 < pl.num_programs(0))
    def _():
        for cp in page_copies(step + 1, (step + 1) % 2):
            cp.start()

    slot = step % 2
    for cp in page_copies(step, slot):
        cp.wait()

    q = q_ref[0, 0]
    q_t = q.T
    r_i = lax.broadcasted_iota(jnp.int32, (SAMPLE_ROWS, SAMPLE_ROWS), 0)
    c_i = lax.broadcasted_iota(jnp.int32, (SAMPLE_ROWS, SAMPLE_ROWS), 1)
    s_own = jnp.where(c_i <= r_i, _mm(q, ko_ref[0, 0], F32, NT), NEG)
    neg_tail = jnp.full((1, LANES - SAMPLE_ROWS), NEG, F32)
    neg_row = jnp.full((SAMPLE_ROWS - SEL_PAGES - 1, LANES), NEG, F32)
    o_cols, p_own, dens = [], [], []
    for t in range(n_tok):
        q_col = q_t[:, t:t + 1]
        rows = [jnp.sum(kbuf[slot, t * SEL_PAGES + r] * q_col, axis=0, keepdims=True) for r in range(SEL_PAGES)]
        rows.append(jnp.concatenate([s_own[t:t + 1], neg_tail], axis=1))
        s_all = jnp.concatenate(rows + [neg_row], axis=0)
        mx = jnp.max(jnp.max(s_all, axis=1, keepdims=True), axis=0, keepdims=True)
        p = jnp.exp(s_all - mx)
        dens.append(jnp.sum(jnp.sum(p, axis=1, keepdims=True), axis=0, keepdims=True))
        acc = vbuf[slot, t * SEL_PAGES] * p[0:1]
        for r in range(1, SEL_PAGES):
            acc = acc + vbuf[slot, t * SEL_PAGES + r] * p[r:r + 1]
        o_cols.append(jnp.sum(acc, axis=1, keepdims=True))
        p_own.append(p[SEL_PAGES:SEL_PAGES + 1, :SAMPLE_ROWS])
    pad = SAMPLE_ROWS - n_tok
    o_sel = jnp.concatenate(o_cols + [jnp.zeros((ATT_HEAD_DIM, pad), F32)], axis=1).T
    o_own = _mm(jnp.concatenate(p_own + [jnp.zeros((pad, SAMPLE_ROWS), F32)], axis=0), vo_ref[0, 0], F32)
    den = jnp.concatenate(dens + [jnp.ones((pad, 1), F32)], axis=0)
    o_ref[0, 0] = (o_sel + o_own) / den


def _sample_attn(page_table_flat, sel_flat, q, k_own, v_own, cache_kt, cache_vt, layer, n_pages, n_tok):
    bn, nh, rows, dh = q.shape
    assert SEL_PAGES + 1 <= SAMPLE_ROWS and rows == SAMPLE_ROWS
    n_buf = n_tok * SEL_PAGES
    own = pl.BlockSpec((1, 1, rows, dh), lambda s, pt, sel: (s // nh, s % nh, 0, 0))
    hbm = pl.BlockSpec(memory_space=pl.ANY)
    return pl.pallas_call(
        functools.partial(_sample_attn_body, layer=layer, n_pages=n_pages, n_tok=n_tok, n_heads=nh),
        grid_spec=pltpu.PrefetchScalarGridSpec(
            num_scalar_prefetch=2,
            grid=(bn * nh,),
            in_specs=[own, own, own, hbm, hbm],
            out_specs=own,
            scratch_shapes=[pltpu.VMEM((2, n_buf, dh, PAGE_SIZE), F32),
                            pltpu.VMEM((2, n_buf, dh, PAGE_SIZE), F32),
                            pltpu.SemaphoreType.DMA((2, 2, n_buf))]),
        out_shape=jax.ShapeDtypeStruct((bn, nh, rows, dh), F32),
        compiler_params=_cparams(("arbitrary",)),
        name="sample_attn",
    )(page_table_flat, sel_flat, q, k_own, v_own, cache_kt, cache_vt)


def _rope_tables(pos):
    half = ATT_HEAD_DIM // 2
    inv = ROPE_THETA ** (-jnp.arange(half, dtype=F32) / half)
    ang = pos.astype(F32)[:, None] * inv[None, :]
    cos = jnp.cos(ang)
    sin = jnp.sin(ang)
    reps = LANES // ATT_HEAD_DIM
    return (jnp.tile(jnp.concatenate([cos, cos], axis=1), (1, reps)),
            jnp.tile(jnp.concatenate([-sin, sin], axis=1), (1, reps)))


def _reorder_w_in(w):
    offs = [0]
    for width in (SSD_D, SSD_CONV_CH, SSD_HEADS, RET_QK, RET_QK, RET_V, RET_V):
        offs.append(offs[-1] + width)
    z, xbc, dt, rq, rk, rv, rg = [w[:, offs[i]:offs[i + 1]] for i in range(7)]
    dt = jnp.pad(dt, ((0, 0), (0, DT_PAD - SSD_HEADS)))
    return jnp.concatenate([z, rv, rg, xbc, rq, rk, dt], axis=1)


def _pad_rows(x, rows):
    return jnp.pad(x, [(0, 0)] * (x.ndim - 2) + [(0, rows - x.shape[-2]), (0, 0)])


def kernel(x_prompt, x_sample, p_prompt, p_sample, state_ssm, state_conv, state_ret, cache_k, cache_v, page_table,
           norm_mix, norm_ffn, norm_ple, ple_proj, ple_gate, hyb_w_in, hyb_w_out, ssd_conv_w, ssd_conv_b,
           ssd_dt_bias, ssd_a_log, ssd_d, ssd_norm, ret_norm, ffn_w_gate, ffn_w_up, ffn_w_down,
           att_w_in, att_w_out, att_q_norm, att_k_norm, moe_router, moe_w_gate, moe_w_up, moe_w_down):
    bp, sp, d = x_prompt.shape
    bs, ts, _ = x_sample.shape
    depth = norm_mix.shape[0]
    n_pages = page_table.shape[1]
    past = n_pages * PAGE_SIZE
    assert sp % MOBA_BLOCK == 0 and past % MOBA_BLOCK == 0 and past >= MOBA_TOPK * MOBA_BLOCK
    assert ts <= SAMPLE_ROWS and math.gcd(ts, SCAN_CHUNK) == ts
    rs = SAMPLE_ROWS

    hp = x_prompt.reshape(bp * sp, d)
    hs = _pad_rows(x_sample, rs).reshape(bs * rs, d)
    pp = p_prompt.reshape(depth, bp * sp, P_DIM)
    ps = _pad_rows(p_sample, rs).reshape(depth, bs * rs, P_DIM)
    cos_p, sin_p = _rope_tables(jnp.arange(sp))
    cos_s, sin_s = _rope_tables(past + jnp.arange(rs))
    pt_flat = page_table.reshape(-1)
    cache_kt = jnp.swapaxes(cache_k, 3, 4)
    cache_vt = jnp.swapaxes(cache_v, 3, 4)
    page_means = _page_mean(cache_kt, 16)

    tm_p, tm_s = 512, bs * rs
    n_chunks = sp // SCAN_CHUNK
    outs = {k: [] for k in ("ssm_p", "conv_p", "ret_p", "k_p", "v_p", "ssm_s", "conv_s", "ret_s", "k_s", "v_s")}
    for i in range(depth):
        j = i // 2
        if i % 2 == 0:
            w_in = _reorder_w_in(hyb_w_in[j])
            w_out = hyb_w_out[j]
            wts = (ssd_conv_w[j], ssd_conv_b[j], ssd_dt_bias[j], ssd_a_log[j], ssd_d[j], ssd_norm[j], ret_norm[j])
            front = SUBLANES - (SSD_CONV - 1)
            u = _norm_matmul(hp, norm_mix[i], w_in.astype(BF16), 2 * tm_p, U_WIDTH // 5)
            mix, ssm, conv, ret = _mixer(
                u, cos_p, sin_p,
                jnp.zeros((bp, SUBLANES, SSD_CONV_CH), F32),
                jnp.zeros((bp, SSD_HEADS, SSD_STATE, SSD_HEAD_DIM), F32),
                jnp.zeros((bp, RET_HEADS, RET_DK, RET_DV), F32), *wts,
                n_seq=bp, n_chunks=n_chunks, q_len=SCAN_CHUNK, n_valid=SCAN_CHUNK, pos_per_chunk=True,
                mxu=BF16)
            hp = _matmul_res(mix, w_out.astype(BF16), hp, tm_p)
            outs["ssm_p"].append(ssm); outs["conv_p"].append(conv[:, front:]); outs["ret_p"].append(ret)
            u = _norm_matmul(hs, norm_mix[i], w_in, tm_s, U_WIDTH // 5)
            mix, ssm, conv, ret = _mixer(
                u, cos_s, sin_s, jnp.pad(state_conv[j], ((0, 0), (front, 0), (0, 0))), state_ssm[j], state_ret[j],
                *wts, n_seq=bs, n_chunks=1, q_len=rs, n_valid=ts, pos_per_chunk=False, mxu=F32)
            hs = _matmul_res(mix, w_out, hs, tm_s)
            outs["ssm_s"].append(ssm); outs["conv_s"].append(conv[:, front:]); outs["ret_s"].append(ret)
            wg, wu, wd = ffn_w_gate[j][None], ffn_w_up[j][None], ffn_w_down[j][None]
            tf = wg.shape[2] // 2
            hp = _ffn(hp, norm_ffn[i], wg.astype(BF16), wu.astype(BF16), wd.astype(BF16), None, tm_p, tf)
            hs = _ffn(hs, norm_ffn[i], wg, wu, wd, None, tm_s, tf)
        else:
            w_in = att_w_in[j]
            w_out = att_w_out[j]
            qkv = _norm_matmul(hp, norm_mix[i], w_in.astype(BF16), 2 * tm_p, ATT_D)
            q, k, v = _qkv_post(qkv, att_q_norm[j], att_k_norm[j], cos_p, sin_p, bp, sp, 256, True)
            hp = _matmul_res(_moba_prompt(q, k, v).reshape(bp * sp, ATT_D), w_out.astype(BF16), hp, tm_p)
            outs["k_p"].append(k); outs["v_p"].append(v)
            qkv = _norm_matmul(hs, norm_mix[i], w_in, tm_s, ATT_D)
            q, k, v = _qkv_post(qkv, att_q_norm[j], att_k_norm[j], cos_s, sin_s, bs, rs, rs, False)
            sel = _sample_select(pt_flat, q, page_means, j, n_pages)
            sel_flat = sel[:, :, :ts, :MOBA_TOPK].reshape(-1)
            o = _sample_attn(pt_flat, sel_flat, q, k, v, cache_kt, cache_vt, j, n_pages, ts)
            o = o.transpose(0, 2, 1, 3).reshape(bs * rs, ATT_D)
            hs = _matmul_res(o, w_out, hs, tm_s)
            outs["k_s"].append(k[:, :, :ts]); outs["v_s"].append(v[:, :, :ts])
            wg, wu, wd = moe_w_gate[j], moe_w_up[j], moe_w_down[j]
            hp = _ffn(hp, norm_ffn[i], wg.astype(BF16), wu.astype(BF16), wd.astype(BF16), moe_router[j], tm_p,
                      wg.shape[2])
            hs = _ffn(hs, norm_ffn[i], wg, wu, wd, moe_router[j], tm_s, wg.shape[2])
        hp = _ple(hp, pp, i, norm_ple[i], ple_gate[i].astype(BF16), ple_proj[i].astype(BF16), tm_p)
        hs = _ple(hs, ps, i, norm_ple[i], ple_gate[i], ple_proj[i], tm_s)

    st = lambda name: jnp.stack(outs[name])
    return (hp.reshape(bp, sp, d), hs.reshape(bs, rs, d)[:, :ts],
            st("ssm_p"), st("conv_p"), st("ret_p"), st("k_p"), st("v_p"),
            st("ssm_s"), st("conv_s"), st("ret_s"), st("k_s"), st("v_s"))
```
